```python
import math
import jax, jax.numpy as jnp
from jax import lax
import numpy as np

D_MODEL = 1024
BATCH = 8
SEQ = 8192
DEPTH = 1
DEC_BATCH = 2
DEC_SEQ = 8192
PAST_LEN = 128

NORM_EPS = 1e-6

SSD_HEADS = 32
SSD_HEAD_DIM = 64
SSD_INNER = SSD_HEADS * SSD_HEAD_DIM
SSD_GROUPS = 4
SSD_STATE = 128
SSD_CONV = 5
SSD_CHUNK = 128
SSD_BC = SSD_GROUPS * SSD_STATE
SSD_CONV_DIM = SSD_INNER + 2 * SSD_BC

ATT_GROUPS = ((128, 1), (512, 4), (2048, 16))
ATT_HEADS_PER_GROUP = 4
ATT_HEADS = ATT_HEADS_PER_GROUP * len(ATT_GROUPS)
ATT_HEAD_DIM = 64
ATT_WIDTH = ATT_HEADS * ATT_HEAD_DIM
ATT_OUT = ATT_HEADS_PER_GROUP * ATT_HEAD_DIM
ATT_QBLOCK = 64
ROPE_DIMS = ATT_HEAD_DIM // 4
ROPE_THETA = 500000.0
NEG_BIG = -1e30

PEER_HEADS = 8
PEER_KEYS = 128
PEER_EXPERTS = PEER_KEYS * PEER_KEYS
PEER_KEY_DIM = 128
PEER_TOPK = 16
PEER_BLOCK = 128

Z_END = SSD_INNER
XBC_END = Z_END + SSD_CONV_DIM
DT_END = XBC_END + 2 * SSD_HEADS
Q_END = DT_END + ATT_WIDTH
K_END = Q_END + ATT_WIDTH
V_END = K_END + ATT_WIDTH
IN_COLS = V_END + 2 * D_MODEL

kernel_name = "hybrid_ssd_dilated_attn_peer_encoder"


def rmsnorm(x, g):
    xf = x.astype(jnp.float32)
    y = xf * lax.rsqrt(jnp.mean(xf * xf, axis=-1, keepdims=True) + NORM_EPS)
    return (y * g.astype(jnp.float32)).astype(x.dtype)


def group_rmsnorm(y, g):
    b, s, _ = y.shape
    yg = y.reshape(b, s, SSD_GROUPS, SSD_INNER // SSD_GROUPS)
    yg = yg * lax.rsqrt(jnp.mean(yg * yg, axis=-1, keepdims=True) + NORM_EPS)
    return yg.reshape(b, s, SSD_INNER) * g.astype(jnp.float32)


def centred_dwconv(x, w, bias):
    pad = (SSD_CONV - 1) // 2
    y = lax.conv_general_dilated(x, w[:, None, :].astype(x.dtype), window_strides=(1,),
                                 padding=[(pad, pad)], dimension_numbers=("NWC", "WIO", "NWC"),
                                 feature_group_count=x.shape[-1])
    return y + bias.astype(x.dtype)


def ssd_chunked(x, dt, a, bm, cm):
    bsz, s, h, p = x.shape
    g, n = bm.shape[2], bm.shape[3]
    r = h // g
    nc = s // SSD_CHUNK
    xdt = (x * dt[..., None]).reshape(bsz, nc, SSD_CHUNK, g, r, p)
    da = jnp.moveaxis((dt * a).reshape(bsz, nc, SSD_CHUNK, g, r), 2, -1)
    cs = jnp.cumsum(da, axis=-1)
    bc = bm.reshape(bsz, nc, SSD_CHUNK, g, n)
    cc = cm.reshape(bsz, nc, SSD_CHUNK, g, n)
    tri = jnp.tril(jnp.ones((SSD_CHUNK, SSD_CHUNK), dtype=bool))
    decay = jnp.exp(jnp.where(tri, cs[..., :, None] - cs[..., None, :], -jnp.inf))
    cb = jnp.einsum("bclgn,bcsgn->bcgls", cc, bc).astype(jnp.float32)
    y_diag = jnp.einsum("bcgrls,bcsgrp->bclgrp", cb[:, :, :, None] * decay, xdt)
    to_end = jnp.exp(cs[..., -1:] - cs)
    states = jnp.einsum("bclgn,bcgrl,bclgrp->bcgrpn", bc, to_end, xdt)
    chunk_decay = jnp.exp(cs[..., -1])

    def step(carry, inp):
        st, dec = inp
        return carry * dec[..., None, None] + st, carry

    init = jnp.zeros((bsz, g, r, p, n), jnp.float32)
    _, prev = lax.scan(step, init, (jnp.moveaxis(states, 1, 0), jnp.moveaxis(chunk_decay, 1, 0)))
    prev = jnp.moveaxis(prev, 0, 1)
    y_off = jnp.einsum("bclgn,bcgrpn,bcgrl->bclgrp", cc, prev, jnp.exp(cs))
    return (y_diag + y_off).reshape(bsz, s, h, p)


def partial_rope(x):
    s = x.shape[1]
    half = ROPE_DIMS // 2
    inv = ROPE_THETA ** (-jnp.arange(half, dtype=jnp.float32) * 2.0 / ROPE_DIMS)
    ang = jnp.arange(s, dtype=jnp.float32)[:, None] * inv[None, :]
    cos = jnp.cos(ang)[None, :, None, :]
    sin = jnp.sin(ang)[None, :, None, :]
    xr = x[..., :ROPE_DIMS].astype(jnp.float32)
    x1, x2 = xr[..., :half], xr[..., half:]
    rot = jnp.concatenate([x1 * cos - x2 * sin, x2 * cos + x1 * sin], axis=-1).astype(x.dtype)
    return jnp.concatenate([rot, x[..., ROPE_DIMS:]], axis=-1)


def banded_attention(q, k, v, radius):
    t, hd = q.shape[-2], q.shape[-1]
    lead = q.shape[:-2]
    nb = -(-t // ATT_QBLOCK)
    tp = nb * ATT_QBLOCK
    span = ATT_QBLOCK + 2 * radius
    lp = [(0, 0)] * len(lead)
    qb = jnp.pad(q, lp + [(0, tp - t), (0, 0)]).reshape(*lead, nb, ATT_QBLOCK, hd)
    kidx = jnp.arange(nb)[:, None] * ATT_QBLOCK + jnp.arange(span)[None, :]
    kvpad = lp + [(radius, tp - t + radius), (0, 0)]
    kb = jnp.pad(k, kvpad)[..., kidx, :]
    vb = jnp.pad(v, kvpad)[..., kidx, :]
    scores = jnp.einsum("...nqd,...nkd->...nqk", qb, kb).astype(jnp.float32) * (hd ** -0.5)
    qpos = (jnp.arange(nb)[:, None] * ATT_QBLOCK + jnp.arange(ATT_QBLOCK)[None, :])[:, :, None]
    kpos = (kidx - radius)[:, None, :]
    valid = (jnp.abs(qpos - kpos) <= radius) & (kpos >= 0) & (kpos < t)
    scores = jnp.where(valid, scores, NEG_BIG)
    m = jnp.max(scores, axis=-1, keepdims=True)
    pr = jnp.exp(scores - m)
    den = jnp.sum(pr, axis=-1)
    o = jnp.einsum("...nqk,...nkd->...nqd", pr, vb.astype(jnp.float32)) / den[..., None]
    lse = m[..., 0] + jnp.log(den)
    o = o.reshape(*lead, tp, hd)[..., :t, :]
    lse = lse.reshape(*lead, tp)[..., :t]
    return o, lse


def dilated_attention(q, k, v, dilation, radius):
    b, s, h, hd = q.shape
    t = s // dilation

    def to_classes(a):
        return a.reshape(b, t, dilation, h, hd).transpose(0, 2, 3, 1, 4)

    o, lse = banded_attention(to_classes(q), to_classes(k), to_classes(v), radius)
    o = o.transpose(0, 3, 1, 2, 4).reshape(b, s, h, hd)
    lse = lse.transpose(0, 3, 1, 2).reshape(b, s, h)
    return o, lse


def token_mixer(h, w_in, conv_w, conv_b, dt_bias, a_log, d_skip, ssd_norm_g, w_a,
                q_norm_g, k_norm_g, w_b, w_o):
    b, s, _ = h.shape
    proj = h @ w_in
    z = proj[..., :Z_END]
    xbc = jax.nn.silu(centred_dwconv(proj[..., Z_END:XBC_END], conv_w, conv_b))
    xs = xbc[..., :SSD_INNER].reshape(b, s, SSD_HEADS, SSD_HEAD_DIM)
    bm = xbc[..., SSD_INNER:SSD_INNER + SSD_BC].reshape(b, s, SSD_GROUPS, SSD_STATE)
    cm = xbc[..., SSD_INNER + SSD_BC:].reshape(b, s, SSD_GROUPS, SSD_STATE)
    dt = jax.nn.softplus(proj[..., XBC_END:DT_END].astype(jnp.float32).reshape(b, s, 2, SSD_HEADS)
                         + dt_bias.astype(jnp.float32))
    a = -jnp.exp(a_log.astype(jnp.float32))
    y_fwd = ssd_chunked(xs, dt[:, :, 0], a[0], bm, cm)
    y_bwd = jnp.flip(ssd_chunked(jnp.flip(xs, 1), jnp.flip(dt[:, :, 1], 1), a[1],
                                 jnp.flip(bm, 1), jnp.flip(cm, 1)), 1)
    y = y_fwd + y_bwd + d_skip.astype(jnp.float32)[:, None] * xs
    y = y.reshape(b, s, SSD_INNER) * jax.nn.silu(z.astype(jnp.float32))
    branch_a = group_rmsnorm(y, ssd_norm_g).astype(h.dtype) @ w_a
    q = proj[..., DT_END:Q_END].reshape(b, s, ATT_HEADS, ATT_HEAD_DIM)
    k = proj[..., Q_END:K_END].reshape(b, s, ATT_HEADS, ATT_HEAD_DIM)
    v = proj[..., K_END:V_END].reshape(b, s, ATT_HEADS, ATT_HEAD_DIM)
    q = partial_rope(rmsnorm(q, q_norm_g))
    k = partial_rope(rmsnorm(k, k_norm_g))
    outs, lses = [], []
    for gi, (window, dil) in enumerate(ATT_GROUPS):
        sl = slice(gi * ATT_HEADS_PER_GROUP, (gi + 1) * ATT_HEADS_PER_GROUP)
        o, l = dilated_attention(q[:, :, sl], k[:, :, sl], v[:, :, sl], dil, window // (2 * dil))
        outs.append(o)
        lses.append(l)
    wts = jax.nn.softmax(jnp.stack(lses, axis=0), axis=0)
    o = jnp.sum(wts[..., None] * jnp.stack(outs, axis=0), axis=0)
    branch_b = o.reshape(b, s, ATT_OUT).astype(h.dtype) @ w_b
    gates = jax.nn.sigmoid(proj[..., V_END:].astype(jnp.float32)).reshape(b, s, 2, D_MODEL)
    merged = (gates[:, :, 0] * branch_a + gates[:, :, 1] * branch_b).astype(h.dtype)
    return merged @ w_o


def peer(h, w_query, sub_keys, expert_u, expert_v):
    b, s, d = h.shape
    t = h.reshape(-1, d)
    ntok = t.shape[0]
    q = (t @ w_query).reshape(ntok, PEER_HEADS, 2, PEER_KEY_DIM)
    sc = jnp.einsum("nhcd,hckd->nhck", q, sub_keys).astype(jnp.float32)
    hv, hi = lax.top_k(sc, PEER_TOPK)
    cand = (hv[:, :, 0, :, None] + hv[:, :, 1, None, :]).reshape(ntok, PEER_HEADS, -1)
    cidx = (hi[:, :, 0, :, None] * PEER_KEYS + hi[:, :, 1, None, :]).reshape(ntok, PEER_HEADS, -1)
    fs, pos = lax.top_k(cand, PEER_TOPK)
    eidx = jnp.take_along_axis(cidx, pos, axis=-1)
    gw = jax.nn.softmax(fs, axis=-1).astype(h.dtype)
    nblk = ntok // PEER_BLOCK

    def block(args):
        xb, eb, gb = args
        act = jax.nn.gelu(jnp.einsum("tkd,td->tk", expert_u[eb], xb), approximate=False)
        return jnp.einsum("tk,tkd->td", gb * act, expert_v[eb])

    out = lax.map(block, (t.reshape(nblk, PEER_BLOCK, d),
                          eidx.reshape(nblk, PEER_BLOCK, PEER_HEADS * PEER_TOPK),
                          gw.reshape(nblk, PEER_BLOCK, PEER_HEADS * PEER_TOPK)))
    return out.reshape(b, s, d)


def encoder_layer(x, norm_mix_g, w_in, conv_w, conv_b, dt_bias, a_log, d_skip, ssd_norm_g, w_a,
                  q_norm_g, k_norm_g, w_b, w_o, norm_ffn_g, w_query, sub_keys, expert_u, expert_v):
    x = x + token_mixer(rmsnorm(x, norm_mix_g), w_in, conv_w, conv_b, dt_bias, a_log, d_skip,
                        ssd_norm_g, w_a, q_norm_g, k_norm_g, w_b, w_o)
    x = x + peer(rmsnorm(x, norm_ffn_g), w_query, sub_keys, expert_u, expert_v)
    return x


def setup_inputs(seed: int = 0) -> dict:
    key = jax.random.key(seed)
    ks = jax.random.split(key, 24)
    f32 = jnp.float32
    L = DEPTH

    def nrm(k, shape, scale):
        return jax.random.normal(k, shape, f32) * scale

    dt0 = jnp.exp(jax.random.uniform(ks[6], (L, 2, SSD_HEADS), f32, math.log(1e-3), math.log(1e-1)))
    dt_bias = dt0 + jnp.log(-jnp.expm1(-dt0))
    a_log = jnp.log(jax.random.uniform(ks[7], (L, 2, SSD_HEADS), f32, 1.0, 16.0))
    return {
        "x_prompt": nrm(ks[0], (BATCH, SEQ, D_MODEL), 1.0),
        "x_sample": nrm(ks[1], (DEC_BATCH, DEC_SEQ, D_MODEL), 1.0),
        "norm_mix_g": 1.0 + nrm(ks[2], (L, D_MODEL), 0.02),
        "w_in": nrm(ks[3], (L, D_MODEL, IN_COLS), D_MODEL ** -0.5),
        "conv_w": nrm(ks[4], (L, SSD_CONV, SSD_CONV_DIM), SSD_CONV ** -0.5),
        "conv_b": nrm(ks[5], (L, SSD_CONV_DIM), 0.01),
        "dt_bias": dt_bias,
        "a_log": a_log,
        "d_skip": 1.0 + nrm(ks[8], (L, SSD_HEADS), 0.1),
        "ssd_norm_g": 1.0 + nrm(ks[9], (L, SSD_INNER), 0.02),
        "w_a": nrm(ks[10], (L, SSD_INNER, D_MODEL), SSD_INNER ** -0.5),
        "q_norm_g": 1.0 + nrm(ks[11], (L, ATT_HEAD_DIM), 0.02),
        "k_norm_g": 1.0 + nrm(ks[12], (L, ATT_HEAD_DIM), 0.02),
        "w_b": nrm(ks[13], (L, ATT_OUT, D_MODEL), ATT_OUT ** -0.5),
        "w_o": nrm(ks[14], (L, D_MODEL, D_MODEL), D_MODEL ** -0.5),
        "norm_ffn_g": 1.0 + nrm(ks[15], (L, D_MODEL), 0.02),
        "w_query": nrm(ks[16], (L, D_MODEL, PEER_HEADS * 2 * PEER_KEY_DIM), D_MODEL ** -0.5),
        "sub_keys": nrm(ks[17], (L, PEER_HEADS, 2, PEER_KEYS, PEER_KEY_DIM), PEER_KEY_DIM ** -0.5),
        "expert_u": nrm(ks[18], (L, PEER_EXPERTS, D_MODEL), D_MODEL ** -0.5),
        "expert_v": nrm(ks[19], (L, PEER_EXPERTS, D_MODEL), PEER_HEADS ** -0.5),
    }


def reference(x_prompt, x_sample, norm_mix_g, w_in, conv_w, conv_b, dt_bias, a_log, d_skip,
              ssd_norm_g, w_a, q_norm_g, k_norm_g, w_b, w_o, norm_ffn_g, w_query, sub_keys,
              expert_u, expert_v):
    def trunk(x):
        for l in range(DEPTH):
            x = encoder_layer(x, norm_mix_g[l], w_in[l], conv_w[l], conv_b[l], dt_bias[l], a_log[l],
                              d_skip[l], ssd_norm_g[l], w_a[l], q_norm_g[l], k_norm_g[l], w_b[l],
                              w_o[l], norm_ffn_g[l], w_query[l], sub_keys[l], expert_u[l], expert_v[l])
        return x

    y_prompt = trunk(x_prompt)
    y_sample = trunk(x_sample)
    return (y_prompt, y_sample)
```

```python
import functools
import math

import jax
import jax.numpy as jnp
from jax import lax
from jax.experimental import pallas as pl
from jax.experimental.pallas import tpu as pltpu

_F32 = jnp.float32
_BF16 = jnp.bfloat16
_HIGHEST = lax.Precision.HIGHEST

_V7X_VMEM_BYTES = 64 * 1024 * 1024
_VMEM_LIMIT = _V7X_VMEM_BYTES - 8 * 1024 * 1024
_LANES = 128

_EPS = 1e-6
_D = 1024
_SSD_HEADS = 32
_SSD_HEAD_DIM = 64
_SSD_INNER = 2048
_SSD_GROUPS = 4
_SSD_STATE = 128
_SSD_BC = 512
_CONV_W = 5
_CONV_DIM = 3072
_CHUNK = 128
_HALO = 16
_ATT_GROUPS = ((128, 1), (512, 4), (2048, 16))
_ATT_WIDTH = 768
_ATT_OUT = 256
_ATT_HD = 64
_ATT_RADIUS = 64
_ROPE_DIMS = 16
_ROPE_THETA = 500000.0
_NEG_BIG = -1e30
_PEER_HEADS = 8
_PEER_KEYS = 128
_PEER_TOPK = 16
_QKV_W = 3 * _ATT_WIDTH
_GATE_W = 2 * _D
_DT_W = 128
_IN_COLS_PAD = _SSD_INNER + _CONV_DIM + _QKV_W + _GATE_W + _DT_W


def _cparams(*sem):
    return pltpu.CompilerParams(dimension_semantics=sem, vmem_limit_bytes=_VMEM_LIMIT)


def _sigmoid(x):
    return 1.0 / (1.0 + jnp.exp(-x))


_COL_CHUNK = 512


def _inproj_body(x_ref, g_ref, w_ref, z_ref, xbc_ref, qkv_ref, gate_ref, dt_ref):
    x = x_ref[...]
    y = x * lax.rsqrt(jnp.mean(x * x, axis=-1, keepdims=True) + _EPS)
    h = (y * g_ref[...]).astype(_BF16)
    col = 0
    for ref in (z_ref, xbc_ref, qkv_ref, gate_ref, dt_ref):
        width = ref.shape[-1]
        for c0 in range(0, width, _COL_CHUNK):
            cw = min(_COL_CHUNK, width - c0)
            acc = jnp.dot(h, w_ref[:, col + c0:col + c0 + cw], preferred_element_type=_F32)
            ref[:, c0:c0 + cw] = acc.astype(ref.dtype)
        col += width


def _in_proj(x2d, g, w_perm, tm):
    n = x2d.shape[0]
    widths = (_SSD_INNER, _CONV_DIM, _QKV_W, _GATE_W, _DT_W)
    dtypes = (_BF16, _BF16, _BF16, _BF16, _F32)
    return pl.pallas_call(
        _inproj_body,
        grid=(n // tm,),
        in_specs=[
            pl.BlockSpec((tm, _D), lambda i: (i, 0)),
            pl.BlockSpec((1, _D), lambda i: (0, 0)),
            pl.BlockSpec((_D, _IN_COLS_PAD), lambda i: (0, 0), pipeline_mode=pl.Buffered(1)),
        ],
        out_specs=[pl.BlockSpec((tm, w), lambda i: (i, 0)) for w in widths],
        out_shape=[jax.ShapeDtypeStruct((n, w), dt) for w, dt in zip(widths, dtypes)],
        compiler_params=_cparams("parallel"),
        name="in_proj",
    )(x2d, g, w_perm)


def _qkprep_body(q_ref, k_ref, qg_ref, kg_ref, c_ref, s1_ref, s2_ref, bd_ref, qo_ref, ko_ref):
    cosv = c_ref[...]
    s1 = s1_ref[...]
    s2 = s2_ref[...]
    bd = bd_ref[...]
    for src, g_ref, dst, scale in ((q_ref, qg_ref, qo_ref, _ATT_HD ** -0.5), (k_ref, kg_ref, ko_ref, 1.0)):
        g = g_ref[...]
        for p in range(_ATT_WIDTH // _LANES):
            x = src[:, p * _LANES:(p + 1) * _LANES].astype(_F32)
            ss = jnp.dot(x * x, bd, precision=_HIGHEST, preferred_element_type=_F32)
            y = x * lax.rsqrt(ss * (1.0 / _ATT_HD) + _EPS) * g
            rot = y * cosv + pltpu.roll(y, 8, 1) * s1 + pltpu.roll(y, _LANES - 8, 1) * s2
            if scale != 1.0:
                rot = rot * scale
            dst[:, p * _LANES:(p + 1) * _LANES] = rot.astype(dst.dtype)


def _rope_tables(seq):
    half = _ROPE_DIMS // 2
    inv = _ROPE_THETA ** (-jnp.arange(half, dtype=_F32) * 2.0 / _ROPE_DIMS)
    ang = jnp.arange(seq, dtype=_F32)[:, None] * inv[None, :]
    cos, sin = jnp.cos(ang), jnp.sin(ang)
    ones = jnp.ones((seq, _ATT_HD - _ROPE_DIMS), _F32)
    zeros = jnp.zeros((seq, _ATT_HD - _ROPE_DIMS), _F32)
    zh = jnp.zeros((seq, half), _F32)
    c = jnp.concatenate([cos, cos, ones], axis=1)
    s1 = jnp.concatenate([zh, sin, zeros], axis=1)
    s2 = jnp.concatenate([-sin, zh, zeros], axis=1)
    rep = _LANES // _ATT_HD
    return tuple(jnp.tile(t, (1, rep)) for t in (c, s1, s2))


def _qk_prep(qkv, q_g, k_g, seq, tm):
    n = qkv.shape[0]
    c, s1, s2 = _rope_tables(seq)
    nseq = seq // tm
    lane = jnp.arange(_LANES)
    bd = (lane[:, None] // _ATT_HD == lane[None, :] // _ATT_HD).astype(_F32)
    rep = _LANES // _ATT_HD
    qg = jnp.tile(q_g.astype(_F32), rep)[None, :]
    kg = jnp.tile(k_g.astype(_F32), rep)[None, :]
    tab = pl.BlockSpec((tm, _LANES), lambda i: (i % nseq, 0))
    const = lambda shape: pl.BlockSpec(shape, lambda i: (0, 0))
    return pl.pallas_call(
        _qkprep_body,
        grid=(n // tm,),
        in_specs=[
            pl.BlockSpec((tm, _ATT_WIDTH), lambda i: (i, 0)),
            pl.BlockSpec((tm, _ATT_WIDTH), lambda i: (i, 1)),
            const((1, _LANES)), const((1, _LANES)), tab, tab, tab, const((_LANES, _LANES)),
        ],
        out_specs=[pl.BlockSpec((tm, _ATT_WIDTH), lambda i: (i, 0))] * 2,
        out_shape=[jax.ShapeDtypeStruct((n, _ATT_WIDTH), _BF16)] * 2,
        compiler_params=_cparams("parallel"),
        name="qk_prep",
    )(qkv, qkv, qg, kg, c, s1, s2, bd)


def _pair_bcast(a, lane_lo, h):
    rows = a.shape[0]
    lo = jnp.broadcast_to(a[:, h:h + 1], (rows, _LANES))
    hi = jnp.broadcast_to(a[:, h + 1:h + 2], (rows, _LANES))
    return jnp.where(lane_lo, lo, hi)


def _ssd_chunk(act_ref, dt, da, h0, rev, state_ref, xsc_ref, emit):
    L = _CHUNK
    row = lax.broadcasted_iota(jnp.int32, (L, L), 0)
    col = lax.broadcasted_iota(jnp.int32, (L, L), 1)
    keep = (row <= col) if rev else (row >= col)
    keep_t = (row >= col) if rev else (row <= col)
    cs = jnp.dot(keep.astype(_F32), da, precision=_HIGHEST, preferred_element_type=_F32)
    cs_t = jnp.dot(da.T, keep_t.astype(_F32), precision=_HIGHEST, preferred_element_type=_F32)
    cs_end = cs[0:1, :] if rev else cs[L - 1:L, :]
    to_end = jnp.exp(cs_end - cs)
    ecs = jnp.exp(cs)
    cdec = jnp.exp(cs_end)
    lane_lo = lax.broadcasted_iota(jnp.int32, (1, _LANES), 1) < _SSD_HEAD_DIM
    heads_per_group = _SSD_HEADS // _SSD_GROUPS
    pairs_per_group = heads_per_group // 2
    gw = heads_per_group * _SSD_HEAD_DIM
    for g in range(_SSD_GROUPS):
        bm = act_ref[:, _SSD_INNER + g * _SSD_STATE:_SSD_INNER + (g + 1) * _SSD_STATE]
        cm = act_ref[:, _SSD_INNER + _SSD_BC + g * _SSD_STATE:_SSD_INNER + _SSD_BC + (g + 1) * _SSD_STATE]
        bm16 = bm.astype(_BF16)
        cm16 = cm.astype(_BF16)
        cb = lax.dot_general(cm16, bm16, (((1,), (1,)), ((), ())), preferred_element_type=_F32)
        y_off = jnp.dot(cm16, state_ref[g].astype(_BF16), preferred_element_type=_F32)
        cdec_cols = []
        for jj in range(pairs_per_group):
            j = g * pairs_per_group + jj
            hl = h0 + 2 * j
            xs = act_ref[:, j * _LANES:(j + 1) * _LANES]
            xdt = xs * _pair_bcast(dt, lane_lo, hl)
            xdt16 = xdt.astype(_BF16)
            res = []
            for u in range(2):
                cb_col = jnp.broadcast_to(cs[:, hl + u:hl + u + 1], (L, L))
                cb_row = jnp.broadcast_to(cs_t[hl + u:hl + u + 1, :], (L, L))
                decay = jnp.exp(jnp.where(keep, cb_col - cb_row, _NEG_BIG))
                res.append(jnp.dot((cb * decay).astype(_BF16), xdt16, preferred_element_type=_F32))
            y = jnp.where(lane_lo, res[0], res[1])
            y = y + y_off[:, jj * _LANES:(jj + 1) * _LANES] * _pair_bcast(ecs, lane_lo, hl)
            emit(j, y)
            xsc_ref[:, j * _LANES:(j + 1) * _LANES] = (xdt * _pair_bcast(to_end, lane_lo, hl)).astype(_BF16)
            cdec_cols.append(_pair_bcast(cdec, lane_lo, hl))
        new = jnp.dot(bm.T.astype(_BF16), xsc_ref[:, g * gw:(g + 1) * gw], preferred_element_type=_F32)
        state_ref[g] = state_ref[g] * jnp.concatenate(cdec_cols, axis=1) + new


def _softplus(x):
    return jnp.maximum(x, 0.0) + jnp.log1p(jnp.exp(-jnp.abs(x)))


def _ssd_fwd_body(xp_ref, xc_ref, xn_ref, dt_ref, cw_ref, cb_ref, dtb_ref, alog_ref, dskip_ref,
                  act_out_ref, y_ref, win_ref, act_ref, state_ref, xsc_ref, *, nchunks):
    c = pl.program_id(1)

    @pl.when(c == 0)
    def _():
        state_ref[...] = jnp.zeros_like(state_ref)

    L = _CHUNK
    prev_ok = (c > 0).astype(_F32)
    next_ok = (c < nchunks - 1).astype(_F32)
    win_ref[0:_HALO, :] = xp_ref[0].astype(_F32) * prev_ok
    win_ref[_HALO:_HALO + L, :] = xc_ref[0].astype(_F32)
    win_ref[_HALO + L:2 * _HALO + L, :] = xn_ref[0].astype(_F32) * next_ok
    pad = (_CONV_W - 1) // 2
    cw = cw_ref[...]
    for c0 in range(0, _CONV_DIM, _COL_CHUNK):
        sl = slice(c0, c0 + _COL_CHUNK)
        acc = jnp.broadcast_to(cb_ref[:, sl], (L, _COL_CHUNK))
        for k in range(_CONV_W):
            acc = acc + win_ref[_HALO - pad + k:_HALO - pad + k + L, sl] * cw[k:k + 1, sl]
        a = acc * _sigmoid(acc)
        act_ref[:, sl] = a
        act_out_ref[0, :, sl] = a.astype(act_out_ref.dtype)
    dt = _softplus(dt_ref[0] + dtb_ref[...])
    da = dt * (-jnp.exp(alog_ref[...]))

    def emit(j, y):
        sl = slice(j * _LANES, (j + 1) * _LANES)
        y_ref[0, :, sl] = y + dskip_ref[:, sl] * act_ref[:, sl]

    _ssd_chunk(act_ref, dt, da, 0, False, state_ref, xsc_ref, emit)


def _ssd_bwd_body(act_in_ref, dt_ref, dtb_ref, alog_ref, yf_ref, z_ref, ng_ref,
                  yn_ref, act_ref, state_ref, xsc_ref, ytot_ref):
    c = pl.program_id(1)

    @pl.when(c == 0)
    def _():
        state_ref[...] = jnp.zeros_like(state_ref)

    act_ref[...] = act_in_ref[0].astype(_F32)
    dt = _softplus(dt_ref[0] + dtb_ref[...])
    da = dt * (-jnp.exp(alog_ref[...]))

    def emit(j, y):
        sl = slice(j * _LANES, (j + 1) * _LANES)
        z = z_ref[0, :, sl].astype(_F32)
        ytot_ref[:, sl] = (y + yf_ref[0, :, sl]) * (z * _sigmoid(z))

    _ssd_chunk(act_ref, dt, da, _SSD_HEADS, True, state_ref, xsc_ref, emit)
    gw = _SSD_INNER // _SSD_GROUPS
    for g in range(_SSD_GROUPS):
        t = ytot_ref[:, g * gw:(g + 1) * gw]
        r = lax.rsqrt(jnp.mean(t * t, axis=-1, keepdims=True) + _EPS)
        yn_ref[0, :, g * gw:(g + 1) * gw] = (t * r * ng_ref[:, g * gw:(g + 1) * gw]).astype(yn_ref.dtype)


def _ssd_scratch():
    return [
        pltpu.VMEM((_CHUNK, _CONV_DIM), _F32),
        pltpu.VMEM((_SSD_GROUPS, _SSD_STATE, _SSD_INNER // _SSD_GROUPS), _F32),
        pltpu.VMEM((_CHUNK, _SSD_INNER), _BF16),
    ]


def _ssd_fwd(xbc, dt, conv_w, conv_b, dtb, alog, dskip, bsz, seq):
    nchunks = seq // _CHUNK
    hb = _CHUNK // _HALO
    xbc3 = xbc.reshape(bsz, seq, _CONV_DIM)
    dt3 = dt.reshape(bsz, seq, _DT_W)
    const = lambda shape: pl.BlockSpec(shape, lambda b, c: (0, 0))
    act, yf = pl.pallas_call(
        functools.partial(_ssd_fwd_body, nchunks=nchunks),
        grid=(bsz, nchunks),
        in_specs=[
            pl.BlockSpec((1, _HALO, _CONV_DIM), lambda b, c: (b, jnp.maximum(c * hb - 1, 0), 0)),
            pl.BlockSpec((1, _CHUNK, _CONV_DIM), lambda b, c: (b, c, 0)),
            pl.BlockSpec((1, _HALO, _CONV_DIM), lambda b, c: (b, jnp.minimum((c + 1) * hb, nchunks * hb - 1), 0)),
            pl.BlockSpec((1, _CHUNK, _DT_W), lambda b, c: (b, c, 0)),
            const((_CONV_W, _CONV_DIM)), const((1, _CONV_DIM)), const((1, _DT_W)), const((1, _DT_W)),
            const((1, _SSD_INNER)),
        ],
        out_specs=[
            pl.BlockSpec((1, _CHUNK, _CONV_DIM), lambda b, c: (b, c, 0)),
            pl.BlockSpec((1, _CHUNK, _SSD_INNER), lambda b, c: (b, c, 0)),
        ],
        out_shape=[
            jax.ShapeDtypeStruct((bsz, seq, _CONV_DIM), _BF16),
            jax.ShapeDtypeStruct((bsz, seq, _SSD_INNER), _F32),
        ],
        scratch_shapes=[pltpu.VMEM((_CHUNK + 2 * _HALO, _CONV_DIM), _F32)] + _ssd_scratch(),
        compiler_params=_cparams("parallel", "arbitrary"),
        name="ssd_fwd",
    )(xbc3, xbc3, xbc3, dt3, conv_w, conv_b, dtb, alog, dskip)
    return act, yf


def _ssd_bwd(act, dt, dtb, alog, yf, z, norm_g, bsz, seq):
    nchunks = seq // _CHUNK
    dt3 = dt.reshape(bsz, seq, _DT_W)
    z3 = z.reshape(bsz, seq, _SSD_INNER)
    const = lambda shape: pl.BlockSpec(shape, lambda b, c: (0, 0))
    blk = lambda w: pl.BlockSpec((1, _CHUNK, w), lambda b, c: (b, nchunks - 1 - c, 0))
    return pl.pallas_call(
        _ssd_bwd_body,
        grid=(bsz, nchunks),
        in_specs=[blk(_CONV_DIM), blk(_DT_W), const((1, _DT_W)), const((1, _DT_W)),
                  blk(_SSD_INNER), blk(_SSD_INNER), const((1, _SSD_INNER))],
        out_specs=blk(_SSD_INNER),
        out_shape=jax.ShapeDtypeStruct((bsz, seq, _SSD_INNER), _BF16),
        scratch_shapes=_ssd_scratch() + [pltpu.VMEM((_CHUNK, _SSD_INNER), _F32)],
        compiler_params=_cparams("parallel", "arbitrary"),
        name="ssd_bwd",
    )(act, dt3, dtb, alog, yf, z3, norm_g)


_QB = 128


def _attn_body(q_ref, kp_ref, kc_ref, kn_ref, vp_ref, vc_ref, vn_ref, o_ref, lse_ref, *, nblk):
    n = pl.program_id(2)
    ii = lax.broadcasted_iota(jnp.int32, (_QB, _QB), 0)
    jj = lax.broadcasted_iota(jnp.int32, (_QB, _QB), 1)
    dlt = jj - ii
    masks = (
        jnp.logical_and(dlt >= _QB - _ATT_RADIUS, n > 0),
        jnp.abs(dlt) <= _ATT_RADIUS,
        jnp.logical_and(dlt <= _ATT_RADIUS - _QB, n < nblk - 1),
    )
    lane_lo = lax.broadcasted_iota(jnp.int32, (1, _LANES), 1) < _ATT_HD
    nt = (((1,), (1,)), ((), ()))
    for p in range(_ATT_OUT // _LANES):
        sl = slice(p * _LANES, (p + 1) * _LANES)
        qp = q_ref[0, :, sl]
        ks = (kp_ref[0, :, sl], kc_ref[0, :, sl], kn_ref[0, :, sl])
        vs = (vp_ref[0, :, sl], vc_ref[0, :, sl], vn_ref[0, :, sl])
        outs, lses = [], []
        for u in range(2):
            hm = lane_lo if u == 0 else jnp.logical_not(lane_lo)
            qm = jnp.where(hm, qp, jnp.zeros_like(qp))
            s = [jnp.where(m, lax.dot_general(qm, kx, nt, preferred_element_type=_F32), _NEG_BIG)
                 for m, kx in zip(masks, ks)]
            mx = jnp.maximum(jnp.maximum(jnp.max(s[0], axis=-1, keepdims=True),
                                         jnp.max(s[1], axis=-1, keepdims=True)),
                             jnp.max(s[2], axis=-1, keepdims=True))
            pr = [jnp.exp(sx - mx) for sx in s]
            den = (jnp.sum(pr[0], axis=-1, keepdims=True) + jnp.sum(pr[1], axis=-1, keepdims=True)
                   + jnp.sum(pr[2], axis=-1, keepdims=True))
            o = sum(jnp.dot(px.astype(_BF16), vx, preferred_element_type=_F32) for px, vx in zip(pr, vs))
            outs.append(o / den)
            lses.append(jnp.broadcast_to(mx + jnp.log(den), (_QB, _LANES)))
        o_ref[0, :, sl] = jnp.where(lane_lo, outs[0], outs[1])
        lse_ref[0, :, sl] = jnp.where(lane_lo, lses[0], lses[1])


def _attention_group(qn, kn, qkv, gi, dil, bsz, seq):
    t = seq // dil
    nblk = t // _QB
    q3 = qn.reshape(bsz, t, dil * _ATT_WIDTH)
    k3 = kn.reshape(bsz, t, dil * _ATT_WIDTH)
    v3 = qkv.reshape(bsz, t, dil * _QKV_W)
    qcols = _ATT_WIDTH // _ATT_OUT
    vcols = _QKV_W // _ATT_OUT
    voff = 2 * qcols + gi

    def spec(cols_per_tok, off, shift):
        def imap(b, r, n):
            return (b, jnp.clip(n + shift, 0, nblk - 1), r * cols_per_tok + off)
        return pl.BlockSpec((1, _QB, _ATT_OUT), imap)

    o, lse = pl.pallas_call(
        functools.partial(_attn_body, nblk=nblk),
        grid=(bsz, dil, nblk),
        in_specs=[spec(qcols, gi, 0), spec(qcols, gi, -1), spec(qcols, gi, 0), spec(qcols, gi, 1),
                  spec(vcols, voff, -1), spec(vcols, voff, 0), spec(vcols, voff, 1)],
        out_specs=[spec(1, 0, 0), spec(1, 0, 0)],
        out_shape=[jax.ShapeDtypeStruct((bsz, t, dil * _ATT_OUT), _F32)] * 2,
        compiler_params=_cparams("parallel", "parallel", "parallel"),
        name=f"attn_d{dil}",
    )(q3, k3, k3, k3, v3, v3, v3)
    return o.reshape(bsz * seq, _ATT_OUT), lse.reshape(bsz * seq, _ATT_OUT)


def _finish_body(yn_ref, o0_ref, l0_ref, o1_ref, l1_ref, o2_ref, l2_ref, gate_ref, x_ref,
                 wa_ref, wb_ref, wo_ref, ng_ref, wq_ref, x1_ref, h2_ref, qp_ref):
    l0, l1, l2 = l0_ref[...], l1_ref[...], l2_ref[...]
    lm = jnp.maximum(jnp.maximum(l0, l1), l2)
    e0, e1, e2 = jnp.exp(l0 - lm), jnp.exp(l1 - lm), jnp.exp(l2 - lm)
    den = e0 + e1 + e2
    o = (e0 / den) * o0_ref[...] + (e1 / den) * o1_ref[...] + (e2 / den) * o2_ref[...]
    a = jnp.dot(yn_ref[...], wa_ref[...], preferred_element_type=_F32)
    b = jnp.dot(o.astype(_BF16), wb_ref[...], preferred_element_type=_F32)
    ga = _sigmoid(gate_ref[:, 0:_D].astype(_F32))
    gb = _sigmoid(gate_ref[:, _D:2 * _D].astype(_F32))
    merged = (ga * a + gb * b).astype(_BF16)
    x1 = x_ref[...] + jnp.dot(merged, wo_ref[...], preferred_element_type=_F32)
    x1_ref[...] = x1
    y = x1 * lax.rsqrt(jnp.mean(x1 * x1, axis=-1, keepdims=True) + _EPS)
    h2 = (y * ng_ref[...]).astype(_BF16)
    h2_ref[...] = h2
    for hc in range(2 * _PEER_HEADS):
        sl = slice(hc * _PEER_KEYS, (hc + 1) * _PEER_KEYS)
        qp_ref[hc] = jnp.dot(h2, wq_ref[:, sl], preferred_element_type=_F32).astype(qp_ref.dtype)


def _finish(yn, attn, gates, x2d, wa, wb, wo, ng, wq, tm):
    n = x2d.shape[0]
    row = lambda w: pl.BlockSpec((tm, w), lambda i: (i, 0))
    const = lambda shape: pl.BlockSpec(shape, lambda i: (0, 0), pipeline_mode=pl.Buffered(1))
    nq = 2 * _PEER_HEADS
    return pl.pallas_call(
        _finish_body,
        grid=(n // tm,),
        in_specs=[row(_SSD_INNER)] + [row(_ATT_OUT)] * 6 + [row(_GATE_W), row(_D),
                  const((_SSD_INNER, _D)), const((_ATT_OUT, _D)), const((_D, _D)), const((1, _D)),
                  const((_D, nq * _PEER_KEYS))],
        out_specs=[row(_D), row(_D), pl.BlockSpec((nq, tm, _PEER_KEYS), lambda i: (0, i, 0))],
        out_shape=[jax.ShapeDtypeStruct((n, _D), _F32), jax.ShapeDtypeStruct((n, _D), _BF16),
                   jax.ShapeDtypeStruct((nq, n, _PEER_KEYS), _BF16)],
        compiler_params=_cparams("parallel"),
        name="finish",
    )(yn, *attn, gates, x2d, wa, wb, wo, ng, wq)


def _top_rows(s, k, key_extra=None):
    rows = s.shape[0]
    key = lax.broadcasted_iota(jnp.int32, s.shape, 0)
    if key_extra is not None:
        key = key * (1 << 14) + key_extra
    big = jnp.int32(2 ** 30)
    vals, keys = [], []
    for _ in range(k):
        m = jnp.max(s, axis=0, keepdims=True)
        sel = jnp.min(jnp.where(s == m, key, big), axis=0, keepdims=True)
        vals.append(m)
        keys.append(sel)
        s = jnp.where(key == sel, -jnp.inf, s)
    return vals, keys


def _route_body(qp_ref, keys_ref, i_ref, j_ref, g_ref, is_ref, js_ref, gs_ref):
    nt = (((1,), (1,)), ((), ()))
    k = _PEER_TOPK

    def head(h, carry):
        hv, hi = [], []
        for c in range(2):
            s = lax.dot_general(keys_ref[2 * h + c], qp_ref[2 * h + c], nt, preferred_element_type=_F32)
            v, i = _top_rows(s, k)
            hv.append(v)
            hi.append(i)
        cand = jnp.concatenate([hv[0][a] + jnp.concatenate(hv[1], axis=0) for a in range(k)], axis=0)
        code = jnp.concatenate([hi[0][a] * _PEER_KEYS + jnp.concatenate(hi[1], axis=0) for a in range(k)], axis=0)
        fs, sel = _top_rows(cand, k, key_extra=code)
        fs = jnp.concatenate(fs, axis=0)
        code = jnp.bitwise_and(jnp.concatenate(sel, axis=0), (1 << 14) - 1)
        e = jnp.exp(fs - fs[0:1, :])
        gw = e / jnp.sum(e, axis=0, keepdims=True)
        r0 = pl.multiple_of(h * k, k)
        is_ref[pl.ds(r0, k), :] = jnp.right_shift(code, 7).astype(_F32)
        js_ref[pl.ds(r0, k), :] = jnp.bitwise_and(code, _PEER_KEYS - 1).astype(_F32)
        gs_ref[pl.ds(r0, k), :] = gw
        return carry

    lax.fori_loop(0, _PEER_HEADS, head, 0)
    i_ref[...] = is_ref[...].T
    j_ref[...] = js_ref[...].T
    g_ref[...] = gs_ref[...].T


def _peer_route(qp, sub_keys16, tq):
    nq, n, _ = qp.shape
    slots = _PEER_HEADS * _PEER_TOPK
    out = pl.BlockSpec((tq, slots), lambda i: (i, 0))
    return pl.pallas_call(
        _route_body,
        grid=(n // tq,),
        in_specs=[pl.BlockSpec((nq, tq, _PEER_KEYS), lambda i: (0, i, 0)),
                  pl.BlockSpec((nq, _PEER_KEYS, _PEER_KEYS), lambda i: (0, 0, 0))],
        out_specs=[out] * 3,
        out_shape=[jax.ShapeDtypeStruct((n, slots), _F32)] * 3,
        scratch_shapes=[pltpu.VMEM((slots, tq), _F32)] * 3,
        compiler_params=_cparams("parallel"),
        name="peer_route",
    )(qp, sub_keys16)


_PEER_TM = 256
_PEER_EB = 512
_PEER_PITCH = _PEER_TM + 8


def _peer_body(h_ref, i_ref, j_ref, g_ref, x1_ref, u_ref, v_ref, o_ref, gate_ref):
    e = pl.program_id(1)
    slots = _PEER_HEADS * _PEER_TOPK
    nt = (((1,), (1,)), ((), ()))

    @pl.when(e == 0)
    def _():
        o_ref[...] = x1_ref[...]
        sub = lax.broadcasted_iota(jnp.int32, (_PEER_KEYS, slots), 0).astype(_F32)

        def tok(n, carry):
            irow = i_ref[pl.ds(n, 1), :]
            jrow = j_ref[pl.ds(n, 1), :]
            grow = g_ref[pl.ds(n, 1), :]
            p_t = jnp.where(sub == irow, 1.0, 0.0).astype(_BF16)
            q_t = jnp.where(sub == jrow, grow, 0.0).astype(_BF16)
            gm = lax.dot_general(p_t, q_t, nt, preferred_element_type=_F32)
            gate_ref[pl.ds(n, _PEER_KEYS, stride=_PEER_PITCH), :] = gm
            return carry

        lax.fori_loop(0, _PEER_TM, tok, 0)

    hmat = lax.dot_general(h_ref[...], u_ref[...], nt, preferred_element_type=_F32)
    act = 0.5 * hmat * (1.0 + lax.erf(hmat * (1.0 / math.sqrt(2.0))))
    rows_per_step = _PEER_EB // _PEER_KEYS
    gate = jnp.concatenate(
        [gate_ref[pl.ds(pl.multiple_of((e * rows_per_step + r) * _PEER_PITCH, 8), _PEER_TM), :]
         for r in range(rows_per_step)], axis=1)
    w = (gate * act).astype(_BF16)
    o_ref[...] += jnp.dot(w, v_ref[...], preferred_element_type=_F32)


def _peer_dense(h2, ri, rj, rg, x1, u16, v16):
    n = h2.shape[0]
    tm, eb = _PEER_TM, _PEER_EB
    slots = _PEER_HEADS * _PEER_TOPK
    nexp = u16.shape[0]
    tok = lambda w: pl.BlockSpec((tm, w), lambda i, e: (i, 0))
    return pl.pallas_call(
        _peer_body,
        grid=(n // tm, nexp // eb),
        in_specs=[tok(_D), tok(slots), tok(slots), tok(slots), tok(_D),
                  pl.BlockSpec((eb, _D), lambda i, e: (e, 0)),
                  pl.BlockSpec((eb, _D), lambda i, e: (e, 0))],
        out_specs=tok(_D),
        out_shape=jax.ShapeDtypeStruct((n, _D), _F32),
        scratch_shapes=[pltpu.VMEM((_PEER_KEYS * _PEER_PITCH, _PEER_KEYS), _F32)],
        compiler_params=_cparams("parallel", "arbitrary"),
        name="peer_dense",
    )(h2, ri, rj, rg, x1, u16, v16)


def _layer(x, norm_mix_g, w_in, conv_w, conv_b, dt_bias, a_log, d_skip, ssd_norm_g, w_a,
           q_norm_g, k_norm_g, w_b, w_o, norm_ffn_g, w_query, sub_keys, expert_u, expert_v):
    bsz, seq, _ = x.shape
    n = bsz * seq
    x2d = x.reshape(n, _D)
    z_end = _SSD_INNER
    xbc_end = z_end + _CONV_DIM
    dt_end = xbc_end + 2 * _SSD_HEADS
    v_end = dt_end + _QKV_W
    w_perm = jnp.concatenate(
        [w_in[:, :xbc_end], w_in[:, dt_end:], w_in[:, xbc_end:dt_end],
         jnp.zeros((_D, _DT_W - 2 * _SSD_HEADS), w_in.dtype)], axis=1).astype(_BF16)
    pad_dt = lambda a: jnp.pad(a.astype(_F32).reshape(1, 2 * _SSD_HEADS), ((0, 0), (0, _DT_W - 2 * _SSD_HEADS)))

    z, xbc, qkv, gates, dt = _in_proj(x2d, norm_mix_g.astype(_F32)[None, :], w_perm, tm=512)
    qn, kn = _qk_prep(qkv, q_norm_g, k_norm_g, seq, tm=512)
    act, yf = _ssd_fwd(xbc, dt, conv_w.astype(_F32), conv_b.astype(_F32)[None, :], pad_dt(dt_bias), pad_dt(a_log),
                       jnp.repeat(d_skip.astype(_F32), _SSD_HEAD_DIM)[None, :], bsz, seq)
    yn = _ssd_bwd(act, dt, pad_dt(dt_bias), pad_dt(a_log), yf, z, ssd_norm_g.astype(_F32)[None, :], bsz, seq)
    attn = []
    for gi, (window, dil) in enumerate(_ATT_GROUPS):
        assert window // (2 * dil) == _ATT_RADIUS
        attn.extend(_attention_group(qn, kn, qkv, gi, dil, bsz, seq))
    x1, h2, qp = _finish(yn.reshape(n, _SSD_INNER), attn, gates, x2d, w_a.astype(_BF16), w_b.astype(_BF16),
                         w_o.astype(_BF16), norm_ffn_g.astype(_F32)[None, :], w_query.astype(_BF16), tm=256)
    keys16 = sub_keys.reshape(2 * _PEER_HEADS, _PEER_KEYS, sub_keys.shape[-1]).astype(_BF16)
    ri, rj, rg = _peer_route(qp, keys16, tq=256)
    out = _peer_dense(h2, ri, rj, rg, x1, expert_u.astype(_BF16), expert_v.astype(_BF16))
    return out.reshape(bsz, seq, _D)


def kernel(x_prompt, x_sample, norm_mix_g, w_in, conv_w, conv_b, dt_bias, a_log, d_skip, ssd_norm_g, w_a,
           q_norm_g, k_norm_g, w_b, w_o, norm_ffn_g, w_query, sub_keys, expert_u, expert_v):
    depth = w_in.shape[0]
    nb = x_prompt.shape[0]
    x = jnp.concatenate([x_prompt, x_sample], axis=0)
    for l in range(depth):
        x = _layer(x, norm_mix_g[l], w_in[l], conv_w[l], conv_b[l], dt_bias[l], a_log[l], d_skip[l],
                   ssd_norm_g[l], w_a[l], q_norm_g[l], k_norm_g[l], w_b[l], w_o[l], norm_ffn_g[l],
                   w_query[l], sub_keys[l], expert_u[l], expert_v[l])
    return (x[:nb], x[nb:])
```

```python
import functools
import math

import jax
import jax.numpy as jnp
from jax import lax
from jax.experimental import pallas as pl
from jax.experimental.pallas import tpu as pltpu

_F32 = jnp.float32
_BF16 = jnp.bfloat16
_HIGHEST = lax.Precision.HIGHEST

_V7X_VMEM_BYTES = 64 * 1024 * 1024
_VMEM_LIMIT = _V7X_VMEM_BYTES - 8 * 1024 * 1024
_LANES = 128

_EPS = 1e-6
_D = 1024
_SSD_HEADS = 32
_SSD_HEAD_DIM = 64
_SSD_INNER = 2048
_SSD_GROUPS = 4
_SSD_STATE = 128
_SSD_BC = 512
_CONV_W = 5
_CONV_DIM = 3072
_CHUNK = 128
_HALO = 16
_ATT_GROUPS = ((128, 1), (512, 4), (2048, 16))
_ATT_WIDTH = 768
_ATT_OUT = 256
_ATT_HD = 64
_ATT_RADIUS = 64
_ROPE_DIMS = 16
_ROPE_THETA = 500000.0
_NEG_BIG = -1e30
_PEER_HEADS = 8
_PEER_KEYS = 128
_PEER_TOPK = 16
_QKV_W = 3 * _ATT_WIDTH
_GATE_W = 2 * _D
_DT_W = 128
_IN_COLS_PAD = _SSD_INNER + _CONV_DIM + _QKV_W + _GATE_W + _DT_W


def _cparams(*sem):
    return pltpu.CompilerParams(dimension_semantics=sem, vmem_limit_bytes=_VMEM_LIMIT)


def _sigmoid(x):
    return 1.0 / (1.0 + jnp.exp(-x))


_COL_CHUNK = 512


def _inproj_body(x_ref, g_ref, w_ref, z_ref, xbc_ref, qkv_ref, gate_ref, dt_ref):
    x = x_ref[...]
    y = x * lax.rsqrt(jnp.mean(x * x, axis=-1, keepdims=True) + _EPS)
    h = (y * g_ref[...]).astype(_BF16)
    col = 0
    for ref in (z_ref, xbc_ref, qkv_ref, gate_ref, dt_ref):
        width = ref.shape[-1]
        for c0 in range(0, width, _COL_CHUNK):
            cw = min(_COL_CHUNK, width - c0)
            acc = jnp.dot(h, w_ref[:, col + c0:col + c0 + cw], preferred_element_type=_F32)
            ref[:, c0:c0 + cw] = acc.astype(ref.dtype)
        col += width


def _in_proj(x2d, g, w_perm, tm):
    n = x2d.shape[0]
    widths = (_SSD_INNER, _CONV_DIM, _QKV_W, _GATE_W, _DT_W)
    dtypes = (_BF16, _BF16, _BF16, _BF16, _F32)
    return pl.pallas_call(
        _inproj_body,
        grid=(n // tm,),
        in_specs=[
            pl.BlockSpec((tm, _D), lambda i: (i, 0)),
            pl.BlockSpec((1, _D), lambda i: (0, 0)),
            pl.BlockSpec((_D, _IN_COLS_PAD), lambda i: (0, 0), pipeline_mode=pl.Buffered(1)),
        ],
        out_specs=[pl.BlockSpec((tm, w), lambda i: (i, 0)) for w in widths],
        out_shape=[jax.ShapeDtypeStruct((n, w), dt) for w, dt in zip(widths, dtypes)],
        compiler_params=_cparams("parallel"),
        name="in_proj",
    )(x2d, g, w_perm)


def _qkprep_body(q_ref, k_ref, qg_ref, kg_ref, c_ref, s1_ref, s2_ref, bd_ref, qo_ref, ko_ref):
    cosv = c_ref[...]
    s1 = s1_ref[...]
    s2 = s2_ref[...]
    bd = bd_ref[...]
    for src, g_ref, dst, scale in ((q_ref, qg_ref, qo_ref, _ATT_HD ** -0.5), (k_ref, kg_ref, ko_ref, 1.0)):
        g = g_ref[...]
        for p in range(_ATT_WIDTH // _LANES):
            x = src[:, p * _LANES:(p + 1) * _LANES].astype(_F32)
            ss = jnp.dot(x * x, bd, precision=_HIGHEST, preferred_element_type=_F32)
            y = x * lax.rsqrt(ss * (1.0 / _ATT_HD) + _EPS) * g
            rot = y * cosv + pltpu.roll(y, 8, 1) * s1 + pltpu.roll(y, _LANES - 8, 1) * s2
            if scale != 1.0:
                rot = rot * scale
            dst[:, p * _LANES:(p + 1) * _LANES] = rot.astype(dst.dtype)


def _rope_tables(seq):
    half = _ROPE_DIMS // 2
    inv = _ROPE_THETA ** (-jnp.arange(half, dtype=_F32) * 2.0 / _ROPE_DIMS)
    ang = jnp.arange(seq, dtype=_F32)[:, None] * inv[None, :]
    cos, sin = jnp.cos(ang), jnp.sin(ang)
    ones = jnp.ones((seq, _ATT_HD - _ROPE_DIMS), _F32)
    zeros = jnp.zeros((seq, _ATT_HD - _ROPE_DIMS), _F32)
    zh = jnp.zeros((seq, half), _F32)
    c = jnp.concatenate([cos, cos, ones], axis=1)
    s1 = jnp.concatenate([zh, sin, zeros], axis=1)
    s2 = jnp.concatenate([-sin, zh, zeros], axis=1)
    rep = _LANES // _ATT_HD
    return tuple(jnp.tile(t, (1, rep)) for t in (c, s1, s2))


def _qk_prep(qkv, q_g, k_g, seq, tm):
    n = qkv.shape[0]
    c, s1, s2 = _rope_tables(seq)
    nseq = seq // tm
    lane = jnp.arange(_LANES)
    bd = (lane[:, None] // _ATT_HD == lane[None, :] // _ATT_HD).astype(_F32)
    rep = _LANES // _ATT_HD
    qg = jnp.tile(q_g.astype(_F32), rep)[None, :]
    kg = jnp.tile(k_g.astype(_F32), rep)[None, :]
    tab = pl.BlockSpec((tm, _LANES), lambda i: (i % nseq, 0))
    const = lambda shape: pl.BlockSpec(shape, lambda i: (0, 0))
    return pl.pallas_call(
        _qkprep_body,
        grid=(n // tm,),
        in_specs=[
            pl.BlockSpec((tm, _ATT_WIDTH), lambda i: (i, 0)),
            pl.BlockSpec((tm, _ATT_WIDTH), lambda i: (i, 1)),
            const((1, _LANES)), const((1, _LANES)), tab, tab, tab, const((_LANES, _LANES)),
        ],
        out_specs=[pl.BlockSpec((tm, _ATT_WIDTH), lambda i: (i, 0))] * 2,
        out_shape=[jax.ShapeDtypeStruct((n, _ATT_WIDTH), _BF16)] * 2,
        compiler_params=_cparams("parallel"),
        name="qk_prep",
    )(qkv, qkv, qg, kg, c, s1, s2, bd)


def _pair_bcast(a, lane_lo, h):
    rows = a.shape[0]
    lo = jnp.broadcast_to(a[:, h:h + 1], (rows, _LANES))
    hi = jnp.broadcast_to(a[:, h + 1:h + 2], (rows, _LANES))
    return jnp.where(lane_lo, lo, hi)


def _ssd_chunk(act_ref, dt, da, h0, rev, state_ref, xsc_ref, emit):
    L = _CHUNK
    row = lax.broadcasted_iota(jnp.int32, (L, L), 0)
    col = lax.broadcasted_iota(jnp.int32, (L, L), 1)
    keep = (row <= col) if rev else (row >= col)
    keep_t = (row >= col) if rev else (row <= col)
    cs = jnp.dot(keep.astype(_F32), da, precision=_HIGHEST, preferred_element_type=_F32)
    cs_t = jnp.dot(da.T, keep_t.astype(_F32), precision=_HIGHEST, preferred_element_type=_F32)
    cs_end = cs[0:1, :] if rev else cs[L - 1:L, :]
    to_end = jnp.exp(cs_end - cs)
    ecs = jnp.exp(cs)
    cdec = jnp.exp(cs_end)
    lane_lo = lax.broadcasted_iota(jnp.int32, (1, _LANES), 1) < _SSD_HEAD_DIM
    heads_per_group = _SSD_HEADS // _SSD_GROUPS
    pairs_per_group = heads_per_group // 2
    gw = heads_per_group * _SSD_HEAD_DIM
    for g in range(_SSD_GROUPS):
        bm = act_ref[:, _SSD_INNER + g * _SSD_STATE:_SSD_INNER + (g + 1) * _SSD_STATE]
        cm = act_ref[:, _SSD_INNER + _SSD_BC + g * _SSD_STATE:_SSD_INNER + _SSD_BC + (g + 1) * _SSD_STATE]
        bm16 = bm.astype(_BF16)
        cm16 = cm.astype(_BF16)
        cb = lax.dot_general(cm16, bm16, (((1,), (1,)), ((), ())), preferred_element_type=_F32)
        y_off = jnp.dot(cm16, state_ref[g].astype(_BF16), preferred_element_type=_F32)
        cdec_cols = []
        for jj in range(pairs_per_group):
            j = g * pairs_per_group + jj
            hl = h0 + 2 * j
            xs = act_ref[:, j * _LANES:(j + 1) * _LANES]
            xdt = xs * _pair_bcast(dt, lane_lo, hl)
            xdt16 = xdt.astype(_BF16)
            res = []
            for u in range(2):
                cb_col = jnp.broadcast_to(cs[:, hl + u:hl + u + 1], (L, L))
                cb_row = jnp.broadcast_to(cs_t[hl + u:hl + u + 1, :], (L, L))
                decay = jnp.exp(jnp.where(keep, cb_col - cb_row, _NEG_BIG))
                res.append(jnp.dot((cb * decay).astype(_BF16), xdt16, preferred_element_type=_F32))
            y = jnp.where(lane_lo, res[0], res[1])
            y = y + y_off[:, jj * _LANES:(jj + 1) * _LANES] * _pair_bcast(ecs, lane_lo, hl)
            emit(j, y)
            xsc_ref[:, j * _LANES:(j + 1) * _LANES] = (xdt * _pair_bcast(to_end, lane_lo, hl)).astype(_BF16)
            cdec_cols.append(_pair_bcast(cdec, lane_lo, hl))
        new = jnp.dot(bm.T.astype(_BF16), xsc_ref[:, g * gw:(g + 1) * gw], preferred_element_type=_F32)
        state_ref[g] = state_ref[g] * jnp.concatenate(cdec_cols, axis=1) + new


def _softplus(x):
    return jnp.maximum(x, 0.0) + jnp.log1p(jnp.exp(-jnp.abs(x)))


def _ssd_fwd_body(xp_ref, xc_ref, xn_ref, dt_ref, cw_ref, cb_ref, dtb_ref, alog_ref, dskip_ref,
                  act_out_ref, y_ref, win_ref, act_ref, state_ref, xsc_ref, *, nchunks):
    c = pl.program_id(1)

    @pl.when(c == 0)
    def _():
        state_ref[...] = jnp.zeros_like(state_ref)

    L = _CHUNK
    prev_ok = (c > 0).astype(_F32)
    next_ok = (c < nchunks - 1).astype(_F32)
    win_ref[0:_HALO, :] = xp_ref[0].astype(_F32) * prev_ok
    win_ref[_HALO:_HALO + L, :] = xc_ref[0].astype(_F32)
    win_ref[_HALO + L:2 * _HALO + L, :] = xn_ref[0].astype(_F32) * next_ok
    pad = (_CONV_W - 1) // 2
    cw = cw_ref[...]
    for c0 in range(0, _CONV_DIM, _COL_CHUNK):
        sl = slice(c0, c0 + _COL_CHUNK)
        acc = jnp.broadcast_to(cb_ref[:, sl], (L, _COL_CHUNK))
        for k in range(_CONV_W):
            acc = acc + win_ref[_HALO - pad + k:_HALO - pad + k + L, sl] * cw[k:k + 1, sl]
        a = acc * _sigmoid(acc)
        act_ref[:, sl] = a
        act_out_ref[0, :, sl] = a.astype(act_out_ref.dtype)
    dt = _softplus(dt_ref[0] + dtb_ref[...])
    da = dt * (-jnp.exp(alog_ref[...]))

    def emit(j, y):
        sl = slice(j * _LANES, (j + 1) * _LANES)
        y_ref[0, :, sl] = y + dskip_ref[:, sl] * act_ref[:, sl]

    _ssd_chunk(act_ref, dt, da, 0, False, state_ref, xsc_ref, emit)


def _ssd_bwd_body(act_in_ref, dt_ref, dtb_ref, alog_ref, yf_ref, z_ref, ng_ref,
                  yn_ref, act_ref, state_ref, xsc_ref, ytot_ref):
    c = pl.program_id(1)

    @pl.when(c == 0)
    def _():
        state_ref[...] = jnp.zeros_like(state_ref)

    act_ref[...] = act_in_ref[0].astype(_F32)
    dt = _softplus(dt_ref[0] + dtb_ref[...])
    da = dt * (-jnp.exp(alog_ref[...]))

    def emit(j, y):
        sl = slice(j * _LANES, (j + 1) * _LANES)
        z = z_ref[0, :, sl].astype(_F32)
        ytot_ref[:, sl] = (y + yf_ref[0, :, sl]) * (z * _sigmoid(z))

    _ssd_chunk(act_ref, dt, da, _SSD_HEADS, True, state_ref, xsc_ref, emit)
    gw = _SSD_INNER // _SSD_GROUPS
    for g in range(_SSD_GROUPS):
        t = ytot_ref[:, g * gw:(g + 1) * gw]
        r = lax.rsqrt(jnp.mean(t * t, axis=-1, keepdims=True) + _EPS)
        yn_ref[0, :, g * gw:(g + 1) * gw] = (t * r * ng_ref[:, g * gw:(g + 1) * gw]).astype(yn_ref.dtype)


def _ssd_scratch():
    return [
        pltpu.VMEM((_CHUNK, _CONV_DIM), _F32),
        pltpu.VMEM((_SSD_GROUPS, _SSD_STATE, _SSD_INNER // _SSD_GROUPS), _F32),
        pltpu.VMEM((_CHUNK, _SSD_INNER), _BF16),
    ]


def _ssd_fwd(xbc, dt, conv_w, conv_b, dtb, alog, dskip, bsz, seq):
    nchunks = seq // _CHUNK
    hb = _CHUNK // _HALO
    xbc3 = xbc.reshape(bsz, seq, _CONV_DIM)
    dt3 = dt.reshape(bsz, seq, _DT_W)
    const = lambda shape: pl.BlockSpec(shape, lambda b, c: (0, 0))
    act, yf = pl.pallas_call(
        functools.partial(_ssd_fwd_body, nchunks=nchunks),
        grid=(bsz, nchunks),
        in_specs=[
            pl.BlockSpec((1, _HALO, _CONV_DIM), lambda b, c: (b, jnp.maximum(c * hb - 1, 0), 0)),
            pl.BlockSpec((1, _CHUNK, _CONV_DIM), lambda b, c: (b, c, 0)),
            pl.BlockSpec((1, _HALO, _CONV_DIM), lambda b, c: (b, jnp.minimum((c + 1) * hb, nchunks * hb - 1), 0)),
            pl.BlockSpec((1, _CHUNK, _DT_W), lambda b, c: (b, c, 0)),
            const((_CONV_W, _CONV_DIM)), const((1, _CONV_DIM)), const((1, _DT_W)), const((1, _DT_W)),
            const((1, _SSD_INNER)),
        ],
        out_specs=[
            pl.BlockSpec((1, _CHUNK, _CONV_DIM), lambda b, c: (b, c, 0)),
            pl.BlockSpec((1, _CHUNK, _SSD_INNER), lambda b, c: (b, c, 0)),
        ],
        out_shape=[
            jax.ShapeDtypeStruct((bsz, seq, _CONV_DIM), _BF16),
            jax.ShapeDtypeStruct((bsz, seq, _SSD_INNER), _F32),
        ],
        scratch_shapes=[pltpu.VMEM((_CHUNK + 2 * _HALO, _CONV_DIM), _F32)] + _ssd_scratch(),
        compiler_params=_cparams("parallel", "arbitrary"),
        name="ssd_fwd",
    )(xbc3, xbc3, xbc3, dt3, conv_w, conv_b, dtb, alog, dskip)
    return act, yf


def _ssd_bwd(act, dt, dtb, alog, yf, z, norm_g, bsz, seq):
    nchunks = seq // _CHUNK
    dt3 = dt.reshape(bsz, seq, _DT_W)
    z3 = z.reshape(bsz, seq, _SSD_INNER)
    const = lambda shape: pl.BlockSpec(shape, lambda b, c: (0, 0))
    blk = lambda w: pl.BlockSpec((1, _CHUNK, w), lambda b, c: (b, nchunks - 1 - c, 0))
    return pl.pallas_call(
        _ssd_bwd_body,
        grid=(bsz, nchunks),
        in_specs=[blk(_CONV_DIM), blk(_DT_W), const((1, _DT_W)), const((1, _DT_W)),
                  blk(_SSD_INNER), blk(_SSD_INNER), const((1, _SSD_INNER))],
        out_specs=blk(_SSD_INNER),
        out_shape=jax.ShapeDtypeStruct((bsz, seq, _SSD_INNER), _BF16),
        scratch_shapes=_ssd_scratch() + [pltpu.VMEM((_CHUNK, _SSD_INNER), _F32)],
        compiler_params=_cparams("parallel", "arbitrary"),
        name="ssd_bwd",
    )(act, dt3, dtb, alog, yf, z3, norm_g)


_QB = 128


def _attn_body(q_ref, kp_ref, kc_ref, kn_ref, vp_ref, vc_ref, vn_ref, o_ref, lse_ref, *, nblk):
    n = pl.program_id(2)
    ii = lax.broadcasted_iota(jnp.int32, (_QB, _QB), 0)
    jj = lax.broadcasted_iota(jnp.int32, (_QB, _QB), 1)
    dlt = jj - ii
    masks = (
        jnp.logical_and(dlt >= _QB - _ATT_RADIUS, n > 0),
        jnp.abs(dlt) <= _ATT_RADIUS,
        jnp.logical_and(dlt <= _ATT_RADIUS - _QB, n < nblk - 1),
    )
    lane_lo = lax.broadcasted_iota(jnp.int32, (1, _LANES), 1) < _ATT_HD
    nt = (((1,), (1,)), ((), ()))
    for p in range(_ATT_OUT // _LANES):
        sl = slice(p * _LANES, (p + 1) * _LANES)
        qp = q_ref[0, :, sl]
        ks = (kp_ref[0, :, sl], kc_ref[0, :, sl], kn_ref[0, :, sl])
        vs = (vp_ref[0, :, sl], vc_ref[0, :, sl], vn_ref[0, :, sl])
        outs, lses = [], []
        for u in range(2):
            hm = lane_lo if u == 0 else jnp.logical_not(lane_lo)
            qm = jnp.where(hm, qp, jnp.zeros_like(qp))
            s = [jnp.where(m, lax.dot_general(qm, kx, nt, preferred_element_type=_F32), _NEG_BIG)
                 for m, kx in zip(masks, ks)]
            mx = jnp.maximum(jnp.maximum(jnp.max(s[0], axis=-1, keepdims=True),
                                         jnp.max(s[1], axis=-1, keepdims=True)),
                             jnp.max(s[2], axis=-1, keepdims=True))
            pr = [jnp.exp(sx - mx) for sx in s]
            den = (jnp.sum(pr[0], axis=-1, keepdims=True) + jnp.sum(pr[1], axis=-1, keepdims=True)
                   + jnp.sum(pr[2], axis=-1, keepdims=True))
            o = sum(jnp.dot(px.astype(_BF16), vx, preferred_element_type=_F32) for px, vx in zip(pr, vs))
            outs.append(o / den)
            lses.append(jnp.broadcast_to(mx + jnp.log(den), (_QB, _LANES)))
        o_ref[0, :, sl] = jnp.where(lane_lo, outs[0], outs[1])
        lse_ref[0, :, sl] = jnp.where(lane_lo, lses[0], lses[1])


def _attention_group(qn, kn, qkv, gi, dil, bsz, seq):
    t = seq // dil
    nblk = t // _QB
    q3 = qn.reshape(bsz, t, dil * _ATT_WIDTH)
    k3 = kn.reshape(bsz, t, dil * _ATT_WIDTH)
    v3 = qkv.reshape(bsz, t, dil * _QKV_W)
    qcols = _ATT_WIDTH // _ATT_OUT
    vcols = _QKV_W // _ATT_OUT
    voff = 2 * qcols + gi

    def spec(cols_per_tok, off, shift):
        def imap(b, r, n):
            return (b, jnp.clip(n + shift, 0, nblk - 1), r * cols_per_tok + off)
        return pl.BlockSpec((1, _QB, _ATT_OUT), imap)

    o, lse = pl.pallas_call(
        functools.partial(_attn_body, nblk=nblk),
        grid=(bsz, dil, nblk),
        in_specs=[spec(qcols, gi, 0), spec(qcols, gi, -1), spec(qcols, gi, 0), spec(qcols, gi, 1),
                  spec(vcols, voff, -1), spec(vcols, voff, 0), spec(vcols, voff, 1)],
        out_specs=[spec(1, 0, 0), spec(1, 0, 0)],
        out_shape=[jax.ShapeDtypeStruct((bsz, t, dil * _ATT_OUT), _F32)] * 2,
        compiler_params=_cparams("parallel", "parallel", "parallel"),
        name=f"attn_d{dil}",
    )(q3, k3, k3, k3, v3, v3, v3)
    return o.reshape(bsz * seq, _ATT_OUT), lse.reshape(bsz * seq, _ATT_OUT)


def _finish_body(yn_ref, o0_ref, l0_ref, o1_ref, l1_ref, o2_ref, l2_ref, gate_ref, x_ref,
                 wa_ref, wb_ref, wo_ref, ng_ref, wq_ref, x1_ref, h2_ref, qp_ref):
    l0, l1, l2 = l0_ref[...], l1_ref[...], l2_ref[...]
    lm = jnp.maximum(jnp.maximum(l0, l1), l2)
    e0, e1, e2 = jnp.exp(l0 - lm), jnp.exp(l1 - lm), jnp.exp(l2 - lm)
    den = e0 + e1 + e2
    o = (e0 / den) * o0_ref[...] + (e1 / den) * o1_ref[...] + (e2 / den) * o2_ref[...]
    a = jnp.dot(yn_ref[...], wa_ref[...], preferred_element_type=_F32)
    b = jnp.dot(o.astype(_BF16), wb_ref[...], preferred_element_type=_F32)
    ga = _sigmoid(gate_ref[:, 0:_D].astype(_F32))
    gb = _sigmoid(gate_ref[:, _D:2 * _D].astype(_F32))
    merged = (ga * a + gb * b).astype(_BF16)
    x1 = x_ref[...] + jnp.dot(merged, wo_ref[...], preferred_element_type=_F32)
    x1_ref[...] = x1
    y = x1 * lax.rsqrt(jnp.mean(x1 * x1, axis=-1, keepdims=True) + _EPS)
    h2 = (y * ng_ref[...]).astype(_BF16)
    h2_ref[...] = h2
    for hc in range(2 * _PEER_HEADS):
        sl = slice(hc * _PEER_KEYS, (hc + 1) * _PEER_KEYS)
        qp_ref[hc] = jnp.dot(h2, wq_ref[:, sl], preferred_element_type=_F32).astype(qp_ref.dtype)


def _finish(yn, attn, gates, x2d, wa, wb, wo, ng, wq, tm):
    n = x2d.shape[0]
    row = lambda w: pl.BlockSpec((tm, w), lambda i: (i, 0))
    const = lambda shape: pl.BlockSpec(shape, lambda i: (0, 0), pipeline_mode=pl.Buffered(1))
    nq = 2 * _PEER_HEADS
    return pl.pallas_call(
        _finish_body,
        grid=(n // tm,),
        in_specs=[row(_SSD_INNER)] + [row(_ATT_OUT)] * 6 + [row(_GATE_W), row(_D),
                  const((_SSD_INNER, _D)), const((_ATT_OUT, _D)), const((_D, _D)), const((1, _D)),
                  const((_D, nq * _PEER_KEYS))],
        out_specs=[row(_D), row(_D), pl.BlockSpec((nq, tm, _PEER_KEYS), lambda i: (0, i, 0))],
        out_shape=[jax.ShapeDtypeStruct((n, _D), _F32), jax.ShapeDtypeStruct((n, _D), _BF16),
                   jax.ShapeDtypeStruct((nq, n, _PEER_KEYS), _BF16)],
        compiler_params=_cparams("parallel"),
        name="finish",
    )(yn, *attn, gates, x2d, wa, wb, wo, ng, wq)


def _top_rows(s, k, key):
    big = jnp.int32(2 ** 30)
    vals, keys = [], []
    for _ in range(k):
        m = jnp.max(s, axis=0, keepdims=True)
        sel = jnp.min(jnp.where(s == m, key, big), axis=0, keepdims=True)
        vals.append(m)
        keys.append(sel)
        s = jnp.where(key == sel, -jnp.inf, s)
    return vals, keys


def _route_body(qp_ref, keys_ref, i_ref, j_ref, g_ref, is_ref, js_ref, gs_ref):
    nt = (((1,), (1,)), ((), ()))
    k = _PEER_TOPK
    tq = qp_ref.shape[1]
    code_bits = 14
    n_b = [k // (a + 1) for a in range(k)]
    pos = jnp.concatenate([lax.broadcasted_iota(jnp.int32, (n_b[a], tq), 0) + a * k for a in range(k)], axis=0)

    def head(h, carry):
        hv, hi = [], []
        for c in range(2):
            s = lax.dot_general(keys_ref[2 * h + c], qp_ref[2 * h + c], nt, preferred_element_type=_F32)
            v, i = _top_rows(s, k, lax.broadcasted_iota(jnp.int32, s.shape, 0))
            hv.append(v)
            hi.append(i)
        hv1 = jnp.concatenate(hv[1], axis=0)
        hi1 = jnp.concatenate(hi[1], axis=0)
        cand = jnp.concatenate([hv[0][a] + hv1[0:n_b[a]] for a in range(k)], axis=0)
        code = jnp.concatenate([hi[0][a] * _PEER_KEYS + hi1[0:n_b[a]] for a in range(k)], axis=0)
        fs, sel = _top_rows(cand, k, pos * (1 << code_bits) + code)
        fs = jnp.concatenate(fs, axis=0)
        code = jnp.bitwise_and(jnp.concatenate(sel, axis=0), (1 << code_bits) - 1)
        e = jnp.exp(fs - fs[0:1, :])
        gw = e / jnp.sum(e, axis=0, keepdims=True)
        r0 = pl.multiple_of(h * k, k)
        is_ref[pl.ds(r0, k), :] = jnp.right_shift(code, 7).astype(_F32)
        js_ref[pl.ds(r0, k), :] = jnp.bitwise_and(code, _PEER_KEYS - 1).astype(_F32)
        gs_ref[pl.ds(r0, k), :] = gw
        return carry

    lax.fori_loop(0, _PEER_HEADS, head, 0)
    i_ref[...] = is_ref[...].T
    j_ref[...] = js_ref[...].T
    g_ref[...] = gs_ref[...].T


def _peer_route(qp, sub_keys16, tq):
    nq, n, _ = qp.shape
    slots = _PEER_HEADS * _PEER_TOPK
    out = pl.BlockSpec((tq, slots), lambda i: (i, 0))
    return pl.pallas_call(
        _route_body,
        grid=(n // tq,),
        in_specs=[pl.BlockSpec((nq, tq, _PEER_KEYS), lambda i: (0, i, 0)),
                  pl.BlockSpec((nq, _PEER_KEYS, _PEER_KEYS), lambda i: (0, 0, 0))],
        out_specs=[out] * 3,
        out_shape=[jax.ShapeDtypeStruct((n, slots), _F32)] * 3,
        scratch_shapes=[pltpu.VMEM((slots, tq), _F32)] * 3,
        compiler_params=_cparams("parallel"),
        name="peer_route",
    )(qp, sub_keys16)


_PEER_TM = 512
_PEER_HALF = _PEER_TM // 2
_PEER_EB = 1024
_PEER_CW = 256
_PEER_PITCH = _PEER_HALF + 8
_PEER_UNROLL = 16


def _peer_body(h_ref, i_ref, j_ref, g_ref, x1_ref, u_ref, v_ref, o_ref, gate_ref, w_ref):
    e = pl.program_id(1)
    slots = _PEER_HEADS * _PEER_TOPK
    nt = (((1,), (1,)), ((), ()))
    hi_mask = jnp.uint32(0xFFFF0000)

    @pl.when(e == 0)
    def _():
        o_ref[...] = x1_ref[...]
        sub = lax.broadcasted_iota(jnp.int32, (_PEER_KEYS, slots), 0).astype(_F32).astype(_BF16)
        one = jnp.ones((), _BF16)
        zero = jnp.zeros((), _BF16)

        def gate_bits(n):
            irow = i_ref[pl.ds(n, 1), :].astype(_BF16)
            jrow = j_ref[pl.ds(n, 1), :].astype(_BF16)
            grow = g_ref[pl.ds(n, 1), :].astype(_BF16)
            p_t = jnp.where(sub == irow, one, zero)
            q_t = jnp.where(sub == jrow, grow, zero)
            gm = lax.dot_general(p_t, q_t, nt, preferred_element_type=_F32)
            return pltpu.bitcast(gm, jnp.uint32)

        def pair(n, carry):
            lo = jnp.right_shift(gate_bits(n), jnp.uint32(16))
            hi = jnp.bitwise_and(gate_bits(n + _PEER_HALF), hi_mask)
            gate_ref[pl.ds(n, _PEER_KEYS, stride=_PEER_PITCH), :] = jnp.bitwise_or(lo, hi)
            return carry

        lax.fori_loop(0, _PEER_HALF, pair, 0, unroll=_PEER_UNROLL)

    rows_per_chunk = _PEER_CW // _PEER_KEYS
    h = h_ref[...]
    for c in range(_PEER_EB // _PEER_CW):
        hmat = lax.dot_general(h, u_ref[c * _PEER_CW:(c + 1) * _PEER_CW, :], nt, preferred_element_type=_F32)
        act = 0.5 * hmat * (1.0 + lax.erf(hmat * (1.0 / math.sqrt(2.0))))
        cols = []
        for r in range(rows_per_chunk):
            row = e * (_PEER_EB // _PEER_KEYS) + c * rows_per_chunk + r
            word = gate_ref[pl.ds(pl.multiple_of(row * _PEER_PITCH, 8), _PEER_HALF), :]
            lo = pltpu.bitcast(jnp.left_shift(word, jnp.uint32(16)), _F32)
            hi = pltpu.bitcast(jnp.bitwise_and(word, hi_mask), _F32)
            cols.append(jnp.concatenate([lo, hi], axis=0))
        gate = jnp.concatenate(cols, axis=1)
        w_ref[:, c * _PEER_CW:(c + 1) * _PEER_CW] = (gate * act).astype(_BF16)
    o_ref[...] += jnp.dot(w_ref[...], v_ref[...], preferred_element_type=_F32)


def _peer_dense(h2, ri, rj, rg, x1, u16, v16):
    n = h2.shape[0]
    tm, eb = _PEER_TM, _PEER_EB
    slots = _PEER_HEADS * _PEER_TOPK
    nexp = u16.shape[0]
    tok = lambda w: pl.BlockSpec((tm, w), lambda i, e: (i, 0))
    return pl.pallas_call(
        _peer_body,
        grid=(n // tm, nexp // eb),
        in_specs=[tok(_D), tok(slots), tok(slots), tok(slots), tok(_D),
                  pl.BlockSpec((eb, _D), lambda i, e: (e, 0)),
                  pl.BlockSpec((eb, _D), lambda i, e: (e, 0))],
        out_specs=tok(_D),
        out_shape=jax.ShapeDtypeStruct((n, _D), _F32),
        scratch_shapes=[pltpu.VMEM((_PEER_KEYS * _PEER_PITCH, _PEER_KEYS), jnp.uint32),
                        pltpu.VMEM((tm, eb), _BF16)],
        compiler_params=_cparams("parallel", "arbitrary"),
        name="peer_dense",
    )(h2, ri, rj, rg, x1, u16, v16)


def _layer(x, norm_mix_g, w_in, conv_w, conv_b, dt_bias, a_log, d_skip, ssd_norm_g, w_a,
           q_norm_g, k_norm_g, w_b, w_o, norm_ffn_g, w_query, sub_keys, expert_u, expert_v):
    bsz, seq, _ = x.shape
    n = bsz * seq
    x2d = x.reshape(n, _D)
    z_end = _SSD_INNER
    xbc_end = z_end + _CONV_DIM
    dt_end = xbc_end + 2 * _SSD_HEADS
    v_end = dt_end + _QKV_W
    w_perm = jnp.concatenate(
        [w_in[:, :xbc_end], w_in[:, dt_end:], w_in[:, xbc_end:dt_end],
         jnp.zeros((_D, _DT_W - 2 * _SSD_HEADS), w_in.dtype)], axis=1).astype(_BF16)
    pad_dt = lambda a: jnp.pad(a.astype(_F32).reshape(1, 2 * _SSD_HEADS), ((0, 0), (0, _DT_W - 2 * _SSD_HEADS)))

    z, xbc, qkv, gates, dt = _in_proj(x2d, norm_mix_g.astype(_F32)[None, :], w_perm, tm=512)
    qn, kn = _qk_prep(qkv, q_norm_g, k_norm_g, seq, tm=512)
    act, yf = _ssd_fwd(xbc, dt, conv_w.astype(_F32), conv_b.astype(_F32)[None, :], pad_dt(dt_bias), pad_dt(a_log),
                       jnp.repeat(d_skip.astype(_F32), _SSD_HEAD_DIM)[None, :], bsz, seq)
    yn = _ssd_bwd(act, dt, pad_dt(dt_bias), pad_dt(a_log), yf, z, ssd_norm_g.astype(_F32)[None, :], bsz, seq)
    attn = []
    for gi, (window, dil) in enumerate(_ATT_GROUPS):
        assert window // (2 * dil) == _ATT_RADIUS
        attn.extend(_attention_group(qn, kn, qkv, gi, dil, bsz, seq))
    x1, h2, qp = _finish(yn.reshape(n, _SSD_INNER), attn, gates, x2d, w_a.astype(_BF16), w_b.astype(_BF16),
                         w_o.astype(_BF16), norm_ffn_g.astype(_F32)[None, :], w_query.astype(_BF16), tm=256)
    keys16 = sub_keys.reshape(2 * _PEER_HEADS, _PEER_KEYS, sub_keys.shape[-1]).astype(_BF16)
    ri, rj, rg = _peer_route(qp, keys16, tq=128)
    out = _peer_dense(h2, ri, rj, rg, x1, expert_u.astype(_BF16), expert_v.astype(_BF16))
    return out.reshape(bsz, seq, _D)


def kernel(x_prompt, x_sample, norm_mix_g, w_in, conv_w, conv_b, dt_bias, a_log, d_skip, ssd_norm_g, w_a,
           q_norm_g, k_norm_g, w_b, w_o, norm_ffn_g, w_query, sub_keys, expert_u, expert_v):
    depth = w_in.shape[0]
    nb = x_prompt.shape[0]
    x = jnp.concatenate([x_prompt, x_sample], axis=0)
    for l in range(depth):
        x = _layer(x, norm_mix_g[l], w_in[l], conv_w[l], conv_b[l], dt_bias[l], a_log[l], d_skip[l],
                   ssd_norm_g[l], w_a[l], q_norm_g[l], k_norm_g[l], w_b[l], w_o[l], norm_ffn_g[l],
                   w_query[l], sub_keys[l], expert_u[l], expert_v[l])
    return (x[:nb], x[nb:])
```

```python
import functools
import math

import jax
import jax.numpy as jnp
from jax import lax
from jax.experimental import pallas as pl
from jax.experimental.pallas import tpu as pltpu

_F32 = jnp.float32
_BF16 = jnp.bfloat16
_HIGHEST = lax.Precision.HIGHEST

_V7X_VMEM_BYTES = 64 * 1024 * 1024
_VMEM_LIMIT = _V7X_VMEM_BYTES - 8 * 1024 * 1024
_LANES = 128

_EPS = 1e-6
_D = 1024
_SSD_HEADS = 32
_SSD_HEAD_DIM = 64
_SSD_INNER = 2048
_SSD_GROUPS = 4
_SSD_STATE = 128
_SSD_BC = 512
_CONV_W = 5
_CONV_DIM = 3072
_CHUNK = 128
_HALO = 16
_ATT_GROUPS = ((128, 1), (512, 4), (2048, 16))
_ATT_WIDTH = 768
_ATT_OUT = 256
_ATT_HD = 64
_ATT_RADIUS = 64
_ROPE_DIMS = 16
_ROPE_THETA = 500000.0
_NEG_BIG = -1e30
_PEER_HEADS = 8
_PEER_KEYS = 128
_PEER_TOPK = 16
_QKV_W = 3 * _ATT_WIDTH
_GATE_W = 2 * _D
_DT_W = 128
_IN_COLS_PAD = _SSD_INNER + _CONV_DIM + _QKV_W + _GATE_W + _DT_W


def _cparams(*sem):
    return pltpu.CompilerParams(dimension_semantics=sem, vmem_limit_bytes=_VMEM_LIMIT)


def _sigmoid(x):
    return 1.0 / (1.0 + jnp.exp(-x))


_COL_CHUNK = 512


def _inproj_body(x_ref, g_ref, w_ref, z_ref, xbc_ref, qkv_ref, gate_ref, dt_ref):
    x = x_ref[...]
    y = x * lax.rsqrt(jnp.mean(x * x, axis=-1, keepdims=True) + _EPS)
    h = (y * g_ref[...]).astype(_BF16)
    col = 0
    for ref in (z_ref, xbc_ref, qkv_ref, gate_ref, dt_ref):
        width = ref.shape[-1]
        for c0 in range(0, width, _COL_CHUNK):
            cw = min(_COL_CHUNK, width - c0)
            acc = jnp.dot(h, w_ref[:, col + c0:col + c0 + cw], preferred_element_type=_F32)
            ref[:, c0:c0 + cw] = acc.astype(ref.dtype)
        col += width


def _in_proj(x2d, g, w_perm, tm):
    n = x2d.shape[0]
    widths = (_SSD_INNER, _CONV_DIM, _QKV_W, _GATE_W, _DT_W)
    dtypes = (_BF16, _BF16, _BF16, _BF16, _F32)
    return pl.pallas_call(
        _inproj_body,
        grid=(n // tm,),
        in_specs=[
            pl.BlockSpec((tm, _D), lambda i: (i, 0)),
            pl.BlockSpec((1, _D), lambda i: (0, 0)),
            pl.BlockSpec((_D, _IN_COLS_PAD), lambda i: (0, 0), pipeline_mode=pl.Buffered(1)),
        ],
        out_specs=[pl.BlockSpec((tm, w), lambda i: (i, 0)) for w in widths],
        out_shape=[jax.ShapeDtypeStruct((n, w), dt) for w, dt in zip(widths, dtypes)],
        compiler_params=_cparams("parallel"),
        name="in_proj",
    )(x2d, g, w_perm)


def _qkprep_body(q_ref, k_ref, v_ref, qg_ref, kg_ref, c_ref, s1_ref, s2_ref, bd_ref, *refs):
    outs, tmp_ref = refs[:-1], refs[-1]
    tm = q_ref.shape[0]
    cosv = c_ref[...]
    s1 = s1_ref[...]
    s2 = s2_ref[...]
    bd = bd_ref[...]
    cols_per_group = _ATT_OUT // _LANES
    for ti, (src, g_ref, scale) in enumerate(((q_ref, qg_ref, _ATT_HD ** -0.5), (k_ref, kg_ref, 1.0), (v_ref, None, 1.0))):
        for p in range(_ATT_WIDTH // _LANES):
            x = src[:, p * _LANES:(p + 1) * _LANES].astype(_F32)
            if g_ref is not None:
                ss = jnp.dot(x * x, bd, precision=_HIGHEST, preferred_element_type=_F32)
                y = x * lax.rsqrt(ss * (1.0 / _ATT_HD) + _EPS) * g_ref[...]
                x = y * cosv + pltpu.roll(y, 8, 1) * s1 + pltpu.roll(y, _LANES - 8, 1) * s2
                if scale != 1.0:
                    x = x * scale
            gi, pl_ = divmod(p, cols_per_group)
            dil = _ATT_GROUPS[gi][1]
            dst = outs[3 * gi + ti]
            lanes = slice(pl_ * _LANES, (pl_ + 1) * _LANES)
            if dil == 1:
                dst[0, 0, :, lanes] = x.astype(dst.dtype)
            else:
                tmp_ref[...] = x
                for r in range(dil):
                    dst[0, r, :, lanes] = tmp_ref[pl.ds(r, tm // dil, stride=dil), :].astype(dst.dtype)


def _rope_tables(seq):
    half = _ROPE_DIMS // 2
    inv = _ROPE_THETA ** (-jnp.arange(half, dtype=_F32) * 2.0 / _ROPE_DIMS)
    ang = jnp.arange(seq, dtype=_F32)[:, None] * inv[None, :]
    cos, sin = jnp.cos(ang), jnp.sin(ang)
    ones = jnp.ones((seq, _ATT_HD - _ROPE_DIMS), _F32)
    zeros = jnp.zeros((seq, _ATT_HD - _ROPE_DIMS), _F32)
    zh = jnp.zeros((seq, half), _F32)
    c = jnp.concatenate([cos, cos, ones], axis=1)
    s1 = jnp.concatenate([zh, sin, zeros], axis=1)
    s2 = jnp.concatenate([-sin, zh, zeros], axis=1)
    rep = _LANES // _ATT_HD
    return tuple(jnp.tile(t, (1, rep)) for t in (c, s1, s2))


def _qk_prep(qkv, q_g, k_g, bsz, seq, tm):
    c, s1, s2 = _rope_tables(seq)
    nseq = seq // tm
    lane = jnp.arange(_LANES)
    bd = (lane[:, None] // _ATT_HD == lane[None, :] // _ATT_HD).astype(_F32)
    rep = _LANES // _ATT_HD
    qg = jnp.tile(q_g.astype(_F32), rep)[None, :]
    kg = jnp.tile(k_g.astype(_F32), rep)[None, :]
    tab = pl.BlockSpec((tm, _LANES), lambda b, i: (i, 0))
    const = lambda shape: pl.BlockSpec(shape, lambda b, i: (0, 0))
    col = lambda cblk: pl.BlockSpec((tm, _ATT_WIDTH), lambda b, i: (b * nseq + i, cblk))
    out_specs, out_shapes = [], []
    for _, dil in _ATT_GROUPS:
        for _ in range(3):
            out_specs.append(pl.BlockSpec((1, dil, tm // dil, _ATT_OUT), lambda b, i: (b, 0, i, 0)))
            out_shapes.append(jax.ShapeDtypeStruct((bsz, dil, seq // dil, _ATT_OUT), _BF16))
    return pl.pallas_call(
        _qkprep_body,
        grid=(bsz, nseq),
        in_specs=[col(0), col(1), col(2), const((1, _LANES)), const((1, _LANES)), tab, tab, tab,
                  const((_LANES, _LANES))],
        out_specs=out_specs,
        out_shape=out_shapes,
        scratch_shapes=[pltpu.VMEM((tm, _LANES), _F32)],
        compiler_params=_cparams("parallel", "parallel"),
        name="qk_prep",
    )(qkv, qkv, qkv, qg, kg, c, s1, s2, bd)


def _pair_bcast(a, lane_lo, h):
    rows = a.shape[0]
    lo = jnp.broadcast_to(a[:, h:h + 1], (rows, _LANES))
    hi = jnp.broadcast_to(a[:, h + 1:h + 2], (rows, _LANES))
    return jnp.where(lane_lo, lo, hi)


def _ssd_chunk(act_ref, dt, da, h0, rev, state_ref, xsc_ref, emit):
    L = _CHUNK
    row = lax.broadcasted_iota(jnp.int32, (L, L), 0)
    col = lax.broadcasted_iota(jnp.int32, (L, L), 1)
    keep = (row <= col) if rev else (row >= col)
    keep_t = (row >= col) if rev else (row <= col)
    cs = jnp.dot(keep.astype(_F32), da, precision=_HIGHEST, preferred_element_type=_F32)
    cs_t = jnp.dot(da.T, keep_t.astype(_F32), precision=_HIGHEST, preferred_element_type=_F32)
    cs_end = cs[0:1, :] if rev else cs[L - 1:L, :]
    to_end = jnp.exp(cs_end - cs)
    ecs = jnp.exp(cs)
    cdec = jnp.exp(cs_end)
    lane_lo = lax.broadcasted_iota(jnp.int32, (1, _LANES), 1) < _SSD_HEAD_DIM
    heads_per_group = _SSD_HEADS // _SSD_GROUPS
    pairs_per_group = heads_per_group // 2
    gw = heads_per_group * _SSD_HEAD_DIM
    for g in range(_SSD_GROUPS):
        bm = act_ref[:, _SSD_INNER + g * _SSD_STATE:_SSD_INNER + (g + 1) * _SSD_STATE]
        cm = act_ref[:, _SSD_INNER + _SSD_BC + g * _SSD_STATE:_SSD_INNER + _SSD_BC + (g + 1) * _SSD_STATE]
        bm16 = bm.astype(_BF16)
        cm16 = cm.astype(_BF16)
        cb = lax.dot_general(cm16, bm16, (((1,), (1,)), ((), ())), preferred_element_type=_F32)
        y_off = jnp.dot(cm16, state_ref[g].astype(_BF16), preferred_element_type=_F32)
        cdec_cols = []
        for jj in range(pairs_per_group):
            j = g * pairs_per_group + jj
            hl = h0 + 2 * j
            xs = act_ref[:, j * _LANES:(j + 1) * _LANES]
            xdt = xs * _pair_bcast(dt, lane_lo, hl)
            xdt16 = xdt.astype(_BF16)
            res = []
            for u in range(2):
                cb_col = jnp.broadcast_to(cs[:, hl + u:hl + u + 1], (L, L))
                cb_row = jnp.broadcast_to(cs_t[hl + u:hl + u + 1, :], (L, L))
                decay = jnp.exp(jnp.where(keep, cb_col - cb_row, _NEG_BIG))
                res.append(jnp.dot((cb * decay).astype(_BF16), xdt16, preferred_element_type=_F32))
            y = jnp.where(lane_lo, res[0], res[1])
            y = y + y_off[:, jj * _LANES:(jj + 1) * _LANES] * _pair_bcast(ecs, lane_lo, hl)
            emit(j, y)
            xsc_ref[:, j * _LANES:(j + 1) * _LANES] = (xdt * _pair_bcast(to_end, lane_lo, hl)).astype(_BF16)
            cdec_cols.append(_pair_bcast(cdec, lane_lo, hl))
        new = jnp.dot(bm.T.astype(_BF16), xsc_ref[:, g * gw:(g + 1) * gw], preferred_element_type=_F32)
        state_ref[g] = state_ref[g] * jnp.concatenate(cdec_cols, axis=1) + new


def _softplus(x):
    return jnp.maximum(x, 0.0) + jnp.log1p(jnp.exp(-jnp.abs(x)))


def _ssd_fwd_body(xp_ref, xc_ref, xn_ref, dt_ref, cw_ref, cb_ref, dtb_ref, alog_ref, dskip_ref,
                  act_out_ref, y_ref, win_ref, act_ref, state_ref, xsc_ref, *, nchunks):
    c = pl.program_id(1)

    @pl.when(c == 0)
    def _():
        state_ref[...] = jnp.zeros_like(state_ref)

    L = _CHUNK
    prev_ok = (c > 0).astype(_F32)
    next_ok = (c < nchunks - 1).astype(_F32)
    win_ref[0:_HALO, :] = xp_ref[0].astype(_F32) * prev_ok
    win_ref[_HALO:_HALO + L, :] = xc_ref[0].astype(_F32)
    win_ref[_HALO + L:2 * _HALO + L, :] = xn_ref[0].astype(_F32) * next_ok
    pad = (_CONV_W - 1) // 2
    cw = cw_ref[...]
    for c0 in range(0, _CONV_DIM, _COL_CHUNK):
        sl = slice(c0, c0 + _COL_CHUNK)
        acc = jnp.broadcast_to(cb_ref[:, sl], (L, _COL_CHUNK))
        for k in range(_CONV_W):
            acc = acc + win_ref[_HALO - pad + k:_HALO - pad + k + L, sl] * cw[k:k + 1, sl]
        a = acc * _sigmoid(acc)
        act_ref[:, sl] = a
        act_out_ref[0, :, sl] = a.astype(act_out_ref.dtype)
    dt = _softplus(dt_ref[0] + dtb_ref[...])
    da = dt * (-jnp.exp(alog_ref[...]))

    def emit(j, y):
        sl = slice(j * _LANES, (j + 1) * _LANES)
        y_ref[0, :, sl] = y + dskip_ref[:, sl] * act_ref[:, sl]

    _ssd_chunk(act_ref, dt, da, 0, False, state_ref, xsc_ref, emit)


def _ssd_bwd_body(act_in_ref, dt_ref, dtb_ref, alog_ref, yf_ref, z_ref, ng_ref,
                  yn_ref, act_ref, state_ref, xsc_ref, ytot_ref):
    c = pl.program_id(1)

    @pl.when(c == 0)
    def _():
        state_ref[...] = jnp.zeros_like(state_ref)

    act_ref[...] = act_in_ref[0].astype(_F32)
    dt = _softplus(dt_ref[0] + dtb_ref[...])
    da = dt * (-jnp.exp(alog_ref[...]))

    def emit(j, y):
        sl = slice(j * _LANES, (j + 1) * _LANES)
        z = z_ref[0, :, sl].astype(_F32)
        ytot_ref[:, sl] = (y + yf_ref[0, :, sl]) * (z * _sigmoid(z))

    _ssd_chunk(act_ref, dt, da, _SSD_HEADS, True, state_ref, xsc_ref, emit)
    gw = _SSD_INNER // _SSD_GROUPS
    for g in range(_SSD_GROUPS):
        t = ytot_ref[:, g * gw:(g + 1) * gw]
        r = lax.rsqrt(jnp.mean(t * t, axis=-1, keepdims=True) + _EPS)
        yn_ref[0, :, g * gw:(g + 1) * gw] = (t * r * ng_ref[:, g * gw:(g + 1) * gw]).astype(yn_ref.dtype)


def _ssd_scratch():
    return [
        pltpu.VMEM((_CHUNK, _CONV_DIM), _F32),
        pltpu.VMEM((_SSD_GROUPS, _SSD_STATE, _SSD_INNER // _SSD_GROUPS), _F32),
        pltpu.VMEM((_CHUNK, _SSD_INNER), _BF16),
    ]


def _ssd_fwd(xbc, dt, conv_w, conv_b, dtb, alog, dskip, bsz, seq):
    nchunks = seq // _CHUNK
    hb = _CHUNK // _HALO
    xbc3 = xbc.reshape(bsz, seq, _CONV_DIM)
    dt3 = dt.reshape(bsz, seq, _DT_W)
    const = lambda shape: pl.BlockSpec(shape, lambda b, c: (0, 0))
    act, yf = pl.pallas_call(
        functools.partial(_ssd_fwd_body, nchunks=nchunks),
        grid=(bsz, nchunks),
        in_specs=[
            pl.BlockSpec((1, _HALO, _CONV_DIM), lambda b, c: (b, jnp.maximum(c * hb - 1, 0), 0)),
            pl.BlockSpec((1, _CHUNK, _CONV_DIM), lambda b, c: (b, c, 0)),
            pl.BlockSpec((1, _HALO, _CONV_DIM), lambda b, c: (b, jnp.minimum((c + 1) * hb, nchunks * hb - 1), 0)),
            pl.BlockSpec((1, _CHUNK, _DT_W), lambda b, c: (b, c, 0)),
            const((_CONV_W, _CONV_DIM)), const((1, _CONV_DIM)), const((1, _DT_W)), const((1, _DT_W)),
            const((1, _SSD_INNER)),
        ],
        out_specs=[
            pl.BlockSpec((1, _CHUNK, _CONV_DIM), lambda b, c: (b, c, 0)),
            pl.BlockSpec((1, _CHUNK, _SSD_INNER), lambda b, c: (b, c, 0)),
        ],
        out_shape=[
            jax.ShapeDtypeStruct((bsz, seq, _CONV_DIM), _BF16),
            jax.ShapeDtypeStruct((bsz, seq, _SSD_INNER), _F32),
        ],
        scratch_shapes=[pltpu.VMEM((_CHUNK + 2 * _HALO, _CONV_DIM), _F32)] + _ssd_scratch(),
        compiler_params=_cparams("parallel", "arbitrary"),
        name="ssd_fwd",
    )(xbc3, xbc3, xbc3, dt3, conv_w, conv_b, dtb, alog, dskip)
    return act, yf


def _ssd_bwd(act, dt, dtb, alog, yf, z, norm_g, bsz, seq):
    nchunks = seq // _CHUNK
    dt3 = dt.reshape(bsz, seq, _DT_W)
    z3 = z.reshape(bsz, seq, _SSD_INNER)
    const = lambda shape: pl.BlockSpec(shape, lambda b, c: (0, 0))
    blk = lambda w: pl.BlockSpec((1, _CHUNK, w), lambda b, c: (b, nchunks - 1 - c, 0))
    return pl.pallas_call(
        _ssd_bwd_body,
        grid=(bsz, nchunks),
        in_specs=[blk(_CONV_DIM), blk(_DT_W), const((1, _DT_W)), const((1, _DT_W)),
                  blk(_SSD_INNER), blk(_SSD_INNER), const((1, _SSD_INNER))],
        out_specs=blk(_SSD_INNER),
        out_shape=jax.ShapeDtypeStruct((bsz, seq, _SSD_INNER), _BF16),
        scratch_shapes=_ssd_scratch() + [pltpu.VMEM((_CHUNK, _SSD_INNER), _F32)],
        compiler_params=_cparams("parallel", "arbitrary"),
        name="ssd_bwd",
    )(act, dt3, dtb, alog, yf, z3, norm_g)


_QB = 128


def _attn_body(q_ref, kp_ref, kc_ref, kn_ref, vp_ref, vc_ref, vn_ref, o_ref, lse_ref, *, nblk):
    n = pl.program_id(2)
    ii = lax.broadcasted_iota(jnp.int32, (_QB, _QB), 0)
    jj = lax.broadcasted_iota(jnp.int32, (_QB, _QB), 1)
    dlt = jj - ii
    masks = (
        jnp.logical_and(dlt >= _QB - _ATT_RADIUS, n > 0),
        jnp.abs(dlt) <= _ATT_RADIUS,
        jnp.logical_and(dlt <= _ATT_RADIUS - _QB, n < nblk - 1),
    )
    lane_lo = lax.broadcasted_iota(jnp.int32, (1, _LANES), 1) < _ATT_HD
    nt = (((1,), (1,)), ((), ()))
    npair = _ATT_OUT // _LANES
    heads = [(p, u) for p in range(npair) for u in range(2)]
    sl = [slice(p * _LANES, (p + 1) * _LANES) for p in range(npair)]
    ks = [(kp_ref[0, 0, :, sl[p]], kc_ref[0, 0, :, sl[p]], kn_ref[0, 0, :, sl[p]]) for p in range(npair)]
    vs = [(vp_ref[0, 0, :, sl[p]], vc_ref[0, 0, :, sl[p]], vn_ref[0, 0, :, sl[p]]) for p in range(npair)]
    scores = []
    for p, u in heads:
        qp = q_ref[0, 0, :, sl[p]]
        hm = lane_lo if u == 0 else jnp.logical_not(lane_lo)
        qm = jnp.where(hm, qp, jnp.zeros_like(qp))
        scores.append([jnp.where(m, lax.dot_general(qm, kx, nt, preferred_element_type=_F32), _NEG_BIG)
                       for m, kx in zip(masks, ks[p])])
    mxs = [jnp.max(jnp.maximum(jnp.maximum(s[0], s[1]), s[2]), axis=-1, keepdims=True) for s in scores]
    probs = [[jnp.exp(sx - mx) for sx in s] for s, mx in zip(scores, mxs)]
    dens = [jnp.sum(pr[0] + pr[1] + pr[2], axis=-1, keepdims=True) for pr in probs]
    outs = []
    for (p, u), pr in zip(heads, probs):
        outs.append(sum(jnp.dot(px.astype(_BF16), vx, preferred_element_type=_F32) for px, vx in zip(pr, vs[p])))
    for p in range(npair):
        o0, o1 = outs[2 * p] / dens[2 * p], outs[2 * p + 1] / dens[2 * p + 1]
        l0 = jnp.broadcast_to(mxs[2 * p] + jnp.log(dens[2 * p]), (_QB, _LANES))
        l1 = jnp.broadcast_to(mxs[2 * p + 1] + jnp.log(dens[2 * p + 1]), (_QB, _LANES))
        o_ref[0, 0, :, sl[p]] = jnp.where(lane_lo, o0, o1)
        lse_ref[0, 0, :, sl[p]] = jnp.where(lane_lo, l0, l1)


def _attention_group(q, k, v, bsz, seq, dil):
    t = seq // dil
    nblk = t // _QB

    def spec(shift):
        return pl.BlockSpec((1, 1, _QB, _ATT_OUT), lambda b, r, n: (b, r, jnp.clip(n + shift, 0, nblk - 1), 0))

    return pl.pallas_call(
        functools.partial(_attn_body, nblk=nblk),
        grid=(bsz, dil, nblk),
        in_specs=[spec(0), spec(-1), spec(0), spec(1), spec(-1), spec(0), spec(1)],
        out_specs=[spec(0), spec(0)],
        out_shape=[jax.ShapeDtypeStruct((bsz, dil, t, _ATT_OUT), _F32)] * 2,
        compiler_params=_cparams("parallel", "parallel", "parallel"),
        name=f"attn_d{dil}",
    )(q, k, k, k, v, v, v)


def _finish_body(yn_ref, o0_ref, l0_ref, o1_ref, l1_ref, o2_ref, l2_ref, gate_ref, x_ref,
                 wa_ref, wb_ref, wo_ref, ng_ref, wq_ref, x1_ref, h2_ref, qp_ref, mix_ref):
    tm = x_ref.shape[0]
    npair = _ATT_OUT // _LANES

    def token_order(ref, slot, dil):
        if dil == 1:
            return [ref[0, 0, :, p * _LANES:(p + 1) * _LANES] for p in range(npair)]
        for p in range(npair):
            for r in range(dil):
                mix_ref[slot * npair + p, pl.ds(r, tm // dil, stride=dil), :] = ref[0, r, :, p * _LANES:(p + 1) * _LANES]
        return [mix_ref[slot * npair + p] for p in range(npair)]

    dils = [d for _, d in _ATT_GROUPS]
    o_parts = [token_order(r, 2 * g, dils[g]) for g, r in enumerate((o0_ref, o1_ref, o2_ref))]
    l_parts = [token_order(r, 2 * g + 1, dils[g]) for g, r in enumerate((l0_ref, l1_ref, l2_ref))]
    halves = []
    for p in range(npair):
        l0, l1, l2 = l_parts[0][p], l_parts[1][p], l_parts[2][p]
        lm = jnp.maximum(jnp.maximum(l0, l1), l2)
        e0, e1, e2 = jnp.exp(l0 - lm), jnp.exp(l1 - lm), jnp.exp(l2 - lm)
        den = e0 + e1 + e2
        halves.append((e0 / den) * o_parts[0][p] + (e1 / den) * o_parts[1][p] + (e2 / den) * o_parts[2][p])
    o = jnp.concatenate(halves, axis=1)
    a = jnp.dot(yn_ref[...], wa_ref[...], preferred_element_type=_F32)
    b = jnp.dot(o.astype(_BF16), wb_ref[...], preferred_element_type=_F32)
    ga = _sigmoid(gate_ref[:, 0:_D].astype(_F32))
    gb = _sigmoid(gate_ref[:, _D:2 * _D].astype(_F32))
    merged = (ga * a + gb * b).astype(_BF16)
    x1 = x_ref[...] + jnp.dot(merged, wo_ref[...], preferred_element_type=_F32)
    x1_ref[...] = x1
    y = x1 * lax.rsqrt(jnp.mean(x1 * x1, axis=-1, keepdims=True) + _EPS)
    h2 = (y * ng_ref[...]).astype(_BF16)
    h2_ref[...] = h2
    for hc in range(2 * _PEER_HEADS):
        sl = slice(hc * _PEER_KEYS, (hc + 1) * _PEER_KEYS)
        qp_ref[hc] = jnp.dot(h2, wq_ref[:, sl], preferred_element_type=_F32).astype(qp_ref.dtype)


def _finish(yn, attn, gates, x2d, wa, wb, wo, ng, wq, seq, tm):
    n = x2d.shape[0]
    nseq = seq // tm
    row = lambda w: pl.BlockSpec((tm, w), lambda i: (i, 0))
    const = lambda shape: pl.BlockSpec(shape, lambda i: (0, 0), pipeline_mode=pl.Buffered(1))
    cls = lambda dil: pl.BlockSpec((1, dil, tm // dil, _ATT_OUT), lambda i: (i // nseq, 0, i % nseq, 0))
    attn_specs = [cls(dil) for _, dil in _ATT_GROUPS for _ in range(2)]
    nq = 2 * _PEER_HEADS
    return pl.pallas_call(
        _finish_body,
        grid=(n // tm,),
        in_specs=[row(_SSD_INNER)] + attn_specs + [row(_GATE_W), row(_D),
                  const((_SSD_INNER, _D)), const((_ATT_OUT, _D)), const((_D, _D)), const((1, _D)),
                  const((_D, nq * _PEER_KEYS))],
        out_specs=[row(_D), row(_D), pl.BlockSpec((nq, tm, _PEER_KEYS), lambda i: (0, i, 0))],
        out_shape=[jax.ShapeDtypeStruct((n, _D), _F32), jax.ShapeDtypeStruct((n, _D), _BF16),
                   jax.ShapeDtypeStruct((nq, n, _PEER_KEYS), _BF16)],
        scratch_shapes=[pltpu.VMEM((2 * len(_ATT_GROUPS) * (_ATT_OUT // _LANES), tm, _LANES), _F32)],
        compiler_params=_cparams("parallel"),
        name="finish",
    )(yn, *attn, gates, x2d, wa, wb, wo, ng, wq)


_ROUTE_TB = 1024
_ROUTE_SUB = _ROUTE_TB // _LANES
_ROUTE_CHAINS = 4


def _sweep_top(ref, nrows, k, codes_ref=None):
    sub = _ROUTE_SUB
    neg = jnp.full((sub, _LANES), -jnp.inf, _F32)
    vals, outs = [], []
    prev = None
    for r in range(k):
        best = [None] * _ROUTE_CHAINS
        for row in range(nrows):
            v = ref[row * sub:(row + 1) * sub, :]
            if prev is not None:
                v = jnp.where(prev == row, neg, v)
                ref[row * sub:(row + 1) * sub, :] = v
            c = row % _ROUTE_CHAINS
            rid = jnp.full((sub, _LANES), row, jnp.int32)
            code = codes_ref[row * sub:(row + 1) * sub, :] if codes_ref is not None else None
            if best[c] is None:
                best[c] = (v, rid, code)
            else:
                bv, bk, bc = best[c]
                better = v > bv
                best[c] = (jnp.maximum(bv, v), jnp.where(better, rid, bk),
                           jnp.where(better, code, bc) if code is not None else None)
        acc = best[0]
        for c in range(1, min(_ROUTE_CHAINS, nrows)):
            av, ak, ac = acc
            bv, bk, bc = best[c]
            take = jnp.logical_or(bv > av, jnp.logical_and(bv == av, bk < ak))
            acc = (jnp.maximum(av, bv), jnp.where(take, bk, ak), jnp.where(take, bc, ac) if ac is not None else None)
        vals.append(acc[0])
        outs.append(acc[2] if codes_ref is not None else acc[1])
        prev = acc[1]
    return vals, outs


def _route_body(qp_ref, keys_ref, i_ref, j_ref, g_ref, s_ref, cand_ref, code_ref, is_ref, js_ref, gs_ref):
    nt = (((1,), (1,)), ((), ()))
    k = _PEER_TOPK
    sub = _ROUTE_SUB
    pairs = [(a, b) for a in range(k) for b in range(k // (a + 1))]

    def head(h, carry):
        hv, hi = [], []
        for c in range(2):
            for a in range(sub):
                tile = lax.dot_general(keys_ref[2 * h + c], qp_ref[2 * h + c, a * _LANES:(a + 1) * _LANES, :], nt,
                                       preferred_element_type=_F32)
                s_ref[c, pl.ds(a, _PEER_KEYS, stride=sub), :] = tile
            v, i = _sweep_top(s_ref.at[c], _PEER_KEYS, k)
            hv.append(v)
            hi.append(i)
        for p, (a, b) in enumerate(pairs):
            cand_ref[p * sub:(p + 1) * sub, :] = hv[0][a] + hv[1][b]
            code_ref[p * sub:(p + 1) * sub, :] = hi[0][a] * _PEER_KEYS + hi[1][b]
        fs, codes = _sweep_top(cand_ref, len(pairs), k, codes_ref=code_ref)
        e = [jnp.exp(f - fs[0]) for f in fs]
        den = e[0]
        for x in e[1:]:
            den = den + x
        for r in range(k):
            r0 = pl.multiple_of((h * k + r) * sub, sub)
            is_ref[pl.ds(r0, sub), :] = jnp.right_shift(codes[r], 7).astype(_F32)
            js_ref[pl.ds(r0, sub), :] = jnp.bitwise_and(codes[r], _PEER_KEYS - 1).astype(_F32)
            gs_ref[pl.ds(r0, sub), :] = e[r] / den
        return carry

    lax.fori_loop(0, _PEER_HEADS, head, 0)
    slots = _PEER_HEADS * k
    for src, dst in ((is_ref, i_ref), (js_ref, j_ref), (gs_ref, g_ref)):
        for a in range(sub):
            dst[a * _LANES:(a + 1) * _LANES, :] = src[pl.ds(a, slots, stride=sub), :].T


def _peer_route(qp, sub_keys16):
    nq, n, _ = qp.shape
    tq = _ROUTE_TB
    slots = _PEER_HEADS * _PEER_TOPK
    ncand = sum(_PEER_TOPK // (a + 1) for a in range(_PEER_TOPK))
    out = pl.BlockSpec((tq, slots), lambda i: (i, 0))
    return pl.pallas_call(
        _route_body,
        grid=(n // tq,),
        in_specs=[pl.BlockSpec((nq, tq, _PEER_KEYS), lambda i: (0, i, 0)),
                  pl.BlockSpec((nq, _PEER_KEYS, _PEER_KEYS), lambda i: (0, 0, 0))],
        out_specs=[out] * 3,
        out_shape=[jax.ShapeDtypeStruct((n, slots), _F32)] * 3,
        scratch_shapes=[pltpu.VMEM((2, _PEER_KEYS * _ROUTE_SUB, _LANES), _F32),
                        pltpu.VMEM((ncand * _ROUTE_SUB, _LANES), _F32),
                        pltpu.VMEM((ncand * _ROUTE_SUB, _LANES), jnp.int32)]
        + [pltpu.VMEM((slots * _ROUTE_SUB, _LANES), _F32)] * 3,
        compiler_params=_cparams("parallel"),
        name="peer_route",
    )(qp, sub_keys16)


_PEER_TM = 512
_PEER_HALF = _PEER_TM // 2
_PEER_EB = 1024
_PEER_CW = 256
_PEER_PITCH = _PEER_HALF + 8
_PEER_UNROLL = 16


def _peer_body(h_ref, i_ref, j_ref, g_ref, x1_ref, u_ref, v_ref, o_ref, gate_ref, w_ref):
    e = pl.program_id(1)
    slots = _PEER_HEADS * _PEER_TOPK
    nt = (((1,), (1,)), ((), ()))
    hi_mask = jnp.uint32(0xFFFF0000)

    @pl.when(e == 0)
    def _():
        o_ref[...] = x1_ref[...]
        sub = lax.broadcasted_iota(jnp.int32, (_PEER_KEYS, slots), 0).astype(_F32).astype(_BF16)
        one = jnp.ones((), _BF16)
        zero = jnp.zeros((), _BF16)

        def gate_bits(n):
            irow = i_ref[pl.ds(n, 1), :].astype(_BF16)
            jrow = j_ref[pl.ds(n, 1), :].astype(_BF16)
            grow = g_ref[pl.ds(n, 1), :].astype(_BF16)
            p_t = jnp.where(sub == irow, one, zero)
            q_t = jnp.where(sub == jrow, grow, zero)
            gm = lax.dot_general(p_t, q_t, nt, preferred_element_type=_F32)
            return pltpu.bitcast(gm, jnp.uint32)

        def pair(n, carry):
            lo = jnp.right_shift(gate_bits(n), jnp.uint32(16))
            hi = jnp.bitwise_and(gate_bits(n + _PEER_HALF), hi_mask)
            gate_ref[pl.ds(n, _PEER_KEYS, stride=_PEER_PITCH), :] = jnp.bitwise_or(lo, hi)
            return carry

        lax.fori_loop(0, _PEER_HALF, pair, 0, unroll=_PEER_UNROLL)

    rows_per_chunk = _PEER_CW // _PEER_KEYS
    h = h_ref[...]
    for c in range(_PEER_EB // _PEER_CW):
        hmat = lax.dot_general(h, u_ref[c * _PEER_CW:(c + 1) * _PEER_CW, :], nt, preferred_element_type=_F32)
        act = 0.5 * hmat * (1.0 + lax.erf(hmat * (1.0 / math.sqrt(2.0))))
        cols = []
        for r in range(rows_per_chunk):
            row = e * (_PEER_EB // _PEER_KEYS) + c * rows_per_chunk + r
            word = gate_ref[pl.ds(pl.multiple_of(row * _PEER_PITCH, 8), _PEER_HALF), :]
            lo = pltpu.bitcast(jnp.left_shift(word, jnp.uint32(16)), _F32)
            hi = pltpu.bitcast(jnp.bitwise_and(word, hi_mask), _F32)
            cols.append(jnp.concatenate([lo, hi], axis=0))
        gate = jnp.concatenate(cols, axis=1)
        w_ref[:, c * _PEER_CW:(c + 1) * _PEER_CW] = (gate * act).astype(_BF16)
    o_ref[...] += jnp.dot(w_ref[...], v_ref[...], preferred_element_type=_F32)


def _peer_dense(h2, ri, rj, rg, x1, u16, v16):
    n = h2.shape[0]
    tm, eb = _PEER_TM, _PEER_EB
    slots = _PEER_HEADS * _PEER_TOPK
    nexp = u16.shape[0]
    tok = lambda w: pl.BlockSpec((tm, w), lambda i, e: (i, 0))
    return pl.pallas_call(
        _peer_body,
        grid=(n // tm, nexp // eb),
        in_specs=[tok(_D), tok(slots), tok(slots), tok(slots), tok(_D),
                  pl.BlockSpec((eb, _D), lambda i, e: (e, 0)),
                  pl.BlockSpec((eb, _D), lambda i, e: (e, 0))],
        out_specs=tok(_D),
        out_shape=jax.ShapeDtypeStruct((n, _D), _F32),
        scratch_shapes=[pltpu.VMEM((_PEER_KEYS * _PEER_PITCH, _PEER_KEYS), jnp.uint32),
                        pltpu.VMEM((tm, eb), _BF16)],
        compiler_params=_cparams("parallel", "arbitrary"),
        name="peer_dense",
    )(h2, ri, rj, rg, x1, u16, v16)


def _layer(x, norm_mix_g, w_in, conv_w, conv_b, dt_bias, a_log, d_skip, ssd_norm_g, w_a,
           q_norm_g, k_norm_g, w_b, w_o, norm_ffn_g, w_query, sub_keys, expert_u, expert_v):
    bsz, seq, _ = x.shape
    n = bsz * seq
    x2d = x.reshape(n, _D)
    z_end = _SSD_INNER
    xbc_end = z_end + _CONV_DIM
    dt_end = xbc_end + 2 * _SSD_HEADS
    v_end = dt_end + _QKV_W
    w_perm = jnp.concatenate(
        [w_in[:, :xbc_end], w_in[:, dt_end:], w_in[:, xbc_end:dt_end],
         jnp.zeros((_D, _DT_W - 2 * _SSD_HEADS), w_in.dtype)], axis=1).astype(_BF16)
    pad_dt = lambda a: jnp.pad(a.astype(_F32).reshape(1, 2 * _SSD_HEADS), ((0, 0), (0, _DT_W - 2 * _SSD_HEADS)))

    z, xbc, qkv, gates, dt = _in_proj(x2d, norm_mix_g.astype(_F32)[None, :], w_perm, tm=512)
    qkv_groups = _qk_prep(qkv, q_norm_g, k_norm_g, bsz, seq, tm=512)
    act, yf = _ssd_fwd(xbc, dt, conv_w.astype(_F32), conv_b.astype(_F32)[None, :], pad_dt(dt_bias), pad_dt(a_log),
                       jnp.repeat(d_skip.astype(_F32), _SSD_HEAD_DIM)[None, :], bsz, seq)
    yn = _ssd_bwd(act, dt, pad_dt(dt_bias), pad_dt(a_log), yf, z, ssd_norm_g.astype(_F32)[None, :], bsz, seq)
    attn = []
    for gi, (window, dil) in enumerate(_ATT_GROUPS):
        assert window // (2 * dil) == _ATT_RADIUS
        attn.extend(_attention_group(*qkv_groups[3 * gi:3 * gi + 3], bsz, seq, dil))
    x1, h2, qp = _finish(yn.reshape(n, _SSD_INNER), attn, gates, x2d, w_a.astype(_BF16), w_b.astype(_BF16),
                         w_o.astype(_BF16), norm_ffn_g.astype(_F32)[None, :], w_query.astype(_BF16), seq, tm=256)
    keys16 = sub_keys.reshape(2 * _PEER_HEADS, _PEER_KEYS, sub_keys.shape[-1]).astype(_BF16)
    ri, rj, rg = _peer_route(qp, keys16)
    out = _peer_dense(h2, ri, rj, rg, x1, expert_u.astype(_BF16), expert_v.astype(_BF16))
    return out.reshape(bsz, seq, _D)


def kernel(x_prompt, x_sample, norm_mix_g, w_in, conv_w, conv_b, dt_bias, a_log, d_skip, ssd_norm_g, w_a,
           q_norm_g, k_norm_g, w_b, w_o, norm_ffn_g, w_query, sub_keys, expert_u, expert_v):
    depth = w_in.shape[0]
    nb = x_prompt.shape[0]
    x = jnp.concatenate([x_prompt, x_sample], axis=0)
    for l in range(depth):
        x = _layer(x, norm_mix_g[l], w_in[l], conv_w[l], conv_b[l], dt_bias[l], a_log[l], d_skip[l],
                   ssd_norm_g[l], w_a[l], q_norm_g[l], k_norm_g[l], w_b[l], w_o[l], norm_ffn_g[l],
                   w_query[l], sub_keys[l], expert_u[l], expert_v[l])
    return (x[:nb], x[nb:])
```

```python
import functools
import math

import jax
import jax.numpy as jnp
from jax import lax
from jax.experimental import pallas as pl
from jax.experimental.pallas import tpu as pltpu

_F32 = jnp.float32
_BF16 = jnp.bfloat16
_HIGHEST = lax.Precision.HIGHEST

_V7X_VMEM_BYTES = 64 * 1024 * 1024
_VMEM_LIMIT = _V7X_VMEM_BYTES - 8 * 1024 * 1024
_LANES = 128

_EPS = 1e-6
_D = 1024
_SSD_HEADS = 32
_SSD_HEAD_DIM = 64
_SSD_INNER = 2048
_SSD_GROUPS = 4
_SSD_STATE = 128
_SSD_BC = 512
_CONV_W = 5
_CONV_DIM = 3072
_CHUNK = 128
_HALO = 16
_ATT_GROUPS = ((128, 1), (512, 4), (2048, 16))
_ATT_WIDTH = 768
_ATT_OUT = 256
_ATT_HD = 64
_ATT_RADIUS = 64
_ROPE_DIMS = 16
_ROPE_THETA = 500000.0
_NEG_BIG = -1e30
_PEER_HEADS = 8
_PEER_KEYS = 128
_PEER_TOPK = 16
_QKV_W = 3 * _ATT_WIDTH
_GATE_W = 2 * _D
_DT_W = 128
_IN_COLS_PAD = _SSD_INNER + _CONV_DIM + _QKV_W + _GATE_W + _DT_W


def _cparams(*sem):
    return pltpu.CompilerParams(dimension_semantics=sem, vmem_limit_bytes=_VMEM_LIMIT)


def _sigmoid(x):
    return 1.0 / (1.0 + jnp.exp(-x))


_COL_CHUNK = 512


def _inproj_body(x_ref, g_ref, w_ref, z_ref, xbc_ref, qkv_ref, gate_ref, dt_ref):
    x = x_ref[...]
    y = x * lax.rsqrt(jnp.mean(x * x, axis=-1, keepdims=True) + _EPS)
    h = (y * g_ref[...]).astype(_BF16)
    col = 0
    for ref in (z_ref, xbc_ref, qkv_ref, gate_ref, dt_ref):
        width = ref.shape[-1]
        for c0 in range(0, width, _COL_CHUNK):
            cw = min(_COL_CHUNK, width - c0)
            acc = jnp.dot(h, w_ref[:, col + c0:col + c0 + cw], preferred_element_type=_F32)
            ref[:, c0:c0 + cw] = acc.astype(ref.dtype)
        col += width


def _in_proj(x2d, g, w_perm, tm):
    n = x2d.shape[0]
    widths = (_SSD_INNER, _CONV_DIM, _QKV_W, _GATE_W, _DT_W)
    dtypes = (_BF16, _BF16, _BF16, _BF16, _F32)
    return pl.pallas_call(
        _inproj_body,
        grid=(n // tm,),
        in_specs=[
            pl.BlockSpec((tm, _D), lambda i: (i, 0)),
            pl.BlockSpec((1, _D), lambda i: (0, 0)),
            pl.BlockSpec((_D, _IN_COLS_PAD), lambda i: (0, 0), pipeline_mode=pl.Buffered(1)),
        ],
        out_specs=[pl.BlockSpec((tm, w), lambda i: (i, 0)) for w in widths],
        out_shape=[jax.ShapeDtypeStruct((n, w), dt) for w, dt in zip(widths, dtypes)],
        compiler_params=_cparams("parallel"),
        name="in_proj",
    )(x2d, g, w_perm)


def _qkprep_body(q_ref, k_ref, v_ref, qg_ref, kg_ref, c_ref, s1_ref, s2_ref, bd_ref, *refs):
    outs, tmp_ref = refs[:-1], refs[-1]
    tm = q_ref.shape[0]
    cosv = c_ref[...]
    s1 = s1_ref[...]
    s2 = s2_ref[...]
    bd = bd_ref[...]
    cols_per_group = _ATT_OUT // _LANES
    for ti, (src, g_ref, scale) in enumerate(((q_ref, qg_ref, _ATT_HD ** -0.5), (k_ref, kg_ref, 1.0), (v_ref, None, 1.0))):
        for p in range(_ATT_WIDTH // _LANES):
            x = src[:, p * _LANES:(p + 1) * _LANES].astype(_F32)
            if g_ref is not None:
                ss = jnp.dot(_split3(x * x), bd, preferred_element_type=_F32)
                y = x * lax.rsqrt(ss * (1.0 / _ATT_HD) + _EPS) * g_ref[...]
                x = y * cosv + pltpu.roll(y, 8, 1) * s1 + pltpu.roll(y, _LANES - 8, 1) * s2
                if scale != 1.0:
                    x = x * scale
            gi, pl_ = divmod(p, cols_per_group)
            dil = _ATT_GROUPS[gi][1]
            dst = outs[3 * gi + ti]
            lanes = slice(pl_ * _LANES, (pl_ + 1) * _LANES)
            if dil == 1:
                dst[0, 0, :, lanes] = x.astype(dst.dtype)
            else:
                tmp_ref[...] = x
                for r in range(dil):
                    dst[0, r, :, lanes] = tmp_ref[pl.ds(r, tm // dil, stride=dil), :].astype(dst.dtype)


def _rope_tables(seq):
    half = _ROPE_DIMS // 2
    inv = _ROPE_THETA ** (-jnp.arange(half, dtype=_F32) * 2.0 / _ROPE_DIMS)
    ang = jnp.arange(seq, dtype=_F32)[:, None] * inv[None, :]
    cos, sin = jnp.cos(ang), jnp.sin(ang)
    ones = jnp.ones((seq, _ATT_HD - _ROPE_DIMS), _F32)
    zeros = jnp.zeros((seq, _ATT_HD - _ROPE_DIMS), _F32)
    zh = jnp.zeros((seq, half), _F32)
    c = jnp.concatenate([cos, cos, ones], axis=1)
    s1 = jnp.concatenate([zh, sin, zeros], axis=1)
    s2 = jnp.concatenate([-sin, zh, zeros], axis=1)
    rep = _LANES // _ATT_HD
    return tuple(jnp.tile(t, (1, rep)) for t in (c, s1, s2))


def _qk_prep(qkv, q_g, k_g, bsz, seq, tm):
    c, s1, s2 = _rope_tables(seq)
    nseq = seq // tm
    lane = jnp.arange(_LANES)
    bd = jnp.tile((lane[:, None] // _ATT_HD == lane[None, :] // _ATT_HD).astype(_BF16), (3, 1))
    rep = _LANES // _ATT_HD
    qg = jnp.tile(q_g.astype(_F32), rep)[None, :]
    kg = jnp.tile(k_g.astype(_F32), rep)[None, :]
    tab = pl.BlockSpec((tm, _LANES), lambda b, i: (i, 0))
    const = lambda shape: pl.BlockSpec(shape, lambda b, i: (0, 0))
    col = lambda cblk: pl.BlockSpec((tm, _ATT_WIDTH), lambda b, i: (b * nseq + i, cblk))
    out_specs, out_shapes = [], []
    for _, dil in _ATT_GROUPS:
        for _ in range(3):
            out_specs.append(pl.BlockSpec((1, dil, tm // dil, _ATT_OUT), lambda b, i: (b, 0, i, 0)))
            out_shapes.append(jax.ShapeDtypeStruct((bsz, dil, seq // dil, _ATT_OUT), _BF16))
    return pl.pallas_call(
        _qkprep_body,
        grid=(bsz, nseq),
        in_specs=[col(0), col(1), col(2), const((1, _LANES)), const((1, _LANES)), tab, tab, tab,
                  const((3 * _LANES, _LANES))],
        out_specs=out_specs,
        out_shape=out_shapes,
        scratch_shapes=[pltpu.VMEM((tm, _LANES), _F32)],
        compiler_params=_cparams("parallel", "parallel"),
        name="qk_prep",
    )(qkv, qkv, qkv, qg, kg, c, s1, s2, bd)


def _pair_bcast(a, lane_lo, h):
    rows = a.shape[0]
    lo = jnp.broadcast_to(a[:, h:h + 1], (rows, _LANES))
    hi = jnp.broadcast_to(a[:, h + 1:h + 2], (rows, _LANES))
    return jnp.where(lane_lo, lo, hi)


def _split3(x):
    hi = x.astype(_BF16)
    r1 = x - hi.astype(_F32)
    mid = r1.astype(_BF16)
    lo = (r1 - mid.astype(_F32)).astype(_BF16)
    return jnp.concatenate([hi, mid, lo], axis=1)


def _ssd_chunk(act_ref, dt, da, rev, epair_ref, state_ref, xsc_ref, emit):
    L = _CHUNK
    row = lax.broadcasted_iota(jnp.int32, (L, L), 0)
    col = lax.broadcasted_iota(jnp.int32, (L, L), 1)
    keep = (row <= col) if rev else (row >= col)
    keep_t = (row >= col) if rev else (row <= col)
    cs = jnp.dot(keep.astype(_F32), da, precision=_HIGHEST, preferred_element_type=_F32)
    cs_t = jnp.dot(da.T, keep_t.astype(_F32), precision=_HIGHEST, preferred_element_type=_F32)
    dt_t = dt.T
    cs_end = cs[0:1, :] if rev else cs[L - 1:L, :]
    cdec = jnp.exp(cs_end)
    ecs = jnp.exp(cs)
    wst = jnp.exp(cs_end - cs) * dt
    lane_lo = lax.broadcasted_iota(jnp.int32, (1, _LANES), 1) < _SSD_HEAD_DIM
    if epair_ref is not None:
        ecs_all = jnp.dot(_split3(ecs), epair_ref[...], preferred_element_type=_F32)
        wst_all = jnp.dot(_split3(wst), epair_ref[...], preferred_element_type=_F32)
        xsc_ref[...] = (act_ref[:, 0:_SSD_INNER] * wst_all).astype(_BF16)
    heads_per_group = _SSD_HEADS // _SSD_GROUPS
    pairs_per_group = heads_per_group // 2
    gw = heads_per_group * _SSD_HEAD_DIM
    for g in range(_SSD_GROUPS):
        bm = act_ref[:, _SSD_INNER + g * _SSD_STATE:_SSD_INNER + (g + 1) * _SSD_STATE]
        cm = act_ref[:, _SSD_INNER + _SSD_BC + g * _SSD_STATE:_SSD_INNER + _SSD_BC + (g + 1) * _SSD_STATE]
        bm16 = bm.astype(_BF16)
        cm16 = cm.astype(_BF16)
        cb = lax.dot_general(cm16, bm16, (((1,), (1,)), ((), ())), preferred_element_type=_F32)
        y_off = jnp.dot(cm16, state_ref[g].astype(_BF16), preferred_element_type=_F32)
        cdec_cols = []
        for jj in range(pairs_per_group):
            j = g * pairs_per_group + jj
            sl = slice(j * _LANES, (j + 1) * _LANES)
            xs = act_ref[:, sl]
            xs16 = xs.astype(_BF16)
            res = []
            for u in range(2):
                h = 2 * j + u
                diff = jnp.broadcast_to(cs[:, h:h + 1], (L, L)) - cs_t[h:h + 1, :]
                decay = jnp.exp(jnp.where(keep, diff, _NEG_BIG))
                m = (cb * decay * dt_t[h:h + 1, :]).astype(_BF16)
                res.append(jnp.dot(m, xs16, preferred_element_type=_F32))
            y = jnp.where(lane_lo, res[0], res[1])
            if epair_ref is not None:
                ec = ecs_all[:, sl]
            else:
                ec = _pair_bcast(ecs, lane_lo, 2 * j)
                xsc_ref[:, sl] = (xs * _pair_bcast(wst, lane_lo, 2 * j)).astype(_BF16)
            emit(j, y + y_off[:, jj * _LANES:(jj + 1) * _LANES] * ec)
            cdec_cols.append(_pair_bcast(cdec, lane_lo, 2 * j))
        new = jnp.dot(bm.T.astype(_BF16), xsc_ref[:, g * gw:(g + 1) * gw], preferred_element_type=_F32)
        state_ref[g] = state_ref[g] * jnp.concatenate(cdec_cols, axis=1) + new


def _softplus(x):
    return jnp.maximum(x, 0.0) + jnp.log(1.0 + jnp.exp(-jnp.abs(x)))


def _ssd_fwd_body(xp_ref, xc_ref, xn_ref, dt_ref, cw_ref, cb_ref, dtb_ref, alog_ref, dskip_ref,
                  act_out_ref, y_ref, win_ref, act_ref, state_ref, xsc_ref, *, nchunks):
    c = pl.program_id(1)

    @pl.when(c == 0)
    def _():
        state_ref[...] = jnp.zeros_like(state_ref)

    L = _CHUNK
    prev_ok = (c > 0).astype(_F32)
    next_ok = (c < nchunks - 1).astype(_F32)
    win_ref[0:_HALO, :] = xp_ref[0].astype(_F32) * prev_ok
    win_ref[_HALO:_HALO + L, :] = xc_ref[0].astype(_F32)
    win_ref[_HALO + L:2 * _HALO + L, :] = xn_ref[0].astype(_F32) * next_ok
    pad = (_CONV_W - 1) // 2
    cw = cw_ref[...]
    for c0 in range(0, _CONV_DIM, _COL_CHUNK):
        sl = slice(c0, c0 + _COL_CHUNK)
        acc = jnp.broadcast_to(cb_ref[:, sl], (L, _COL_CHUNK))
        for k in range(_CONV_W):
            acc = acc + win_ref[_HALO - pad + k:_HALO - pad + k + L, sl] * cw[k:k + 1, sl]
        a = acc * _sigmoid(acc)
        act_ref[:, sl] = a
        act_out_ref[0, :, sl] = a.astype(act_out_ref.dtype)
    dt = _softplus(dt_ref[0] + dtb_ref[...])
    da = dt * (-jnp.exp(alog_ref[...]))

    def emit(j, y):
        sl = slice(j * _LANES, (j + 1) * _LANES)
        y_ref[0, :, sl] = y + dskip_ref[:, sl] * act_ref[:, sl]

    _ssd_chunk(act_ref, dt, da, False, None, state_ref, xsc_ref, emit)


def _ssd_bwd_body(act_in_ref, dt_ref, dtb_ref, alog_ref, yf_ref, z_ref, ng_ref, epair_ref,
                  yn_ref, act_ref, state_ref, xsc_ref, ytot_ref):
    c = pl.program_id(1)

    @pl.when(c == 0)
    def _():
        state_ref[...] = jnp.zeros_like(state_ref)

    act_ref[...] = act_in_ref[0].astype(_F32)
    shift = _LANES - _SSD_HEADS
    dt = _softplus(pltpu.roll(dt_ref[0], shift, 1) + pltpu.roll(dtb_ref[...], shift, 1))
    da = dt * (-jnp.exp(pltpu.roll(alog_ref[...], shift, 1)))

    def emit(j, y):
        sl = slice(j * _LANES, (j + 1) * _LANES)
        z = z_ref[0, :, sl].astype(_F32)
        ytot_ref[:, sl] = (y + yf_ref[0, :, sl]) * (z * _sigmoid(z))

    _ssd_chunk(act_ref, dt, da, True, epair_ref, state_ref, xsc_ref, emit)
    gw = _SSD_INNER // _SSD_GROUPS
    for g in range(_SSD_GROUPS):
        t = ytot_ref[:, g * gw:(g + 1) * gw]
        r = lax.rsqrt(jnp.mean(t * t, axis=-1, keepdims=True) + _EPS)
        yn_ref[0, :, g * gw:(g + 1) * gw] = (t * r * ng_ref[:, g * gw:(g + 1) * gw]).astype(yn_ref.dtype)


def _spread_matrix():
    r = jnp.arange(3 * _LANES)[:, None] % _LANES
    return (r == jnp.arange(_SSD_INNER)[None, :] // _SSD_HEAD_DIM).astype(_BF16)


def _ssd_scratch():
    return [
        pltpu.VMEM((_CHUNK, _CONV_DIM), _F32),
        pltpu.VMEM((_SSD_GROUPS, _SSD_STATE, _SSD_INNER // _SSD_GROUPS), _F32),
        pltpu.VMEM((_CHUNK, _SSD_INNER), _BF16),
    ]


def _ssd_fwd(xbc, dt, conv_w, conv_b, dtb, alog, dskip, bsz, seq):
    nchunks = seq // _CHUNK
    hb = _CHUNK // _HALO
    xbc3 = xbc.reshape(bsz, seq, _CONV_DIM)
    dt3 = dt.reshape(bsz, seq, _DT_W)
    const = lambda shape: pl.BlockSpec(shape, lambda b, c: (0,) * len(shape))
    act, yf = pl.pallas_call(
        functools.partial(_ssd_fwd_body, nchunks=nchunks),
        grid=(bsz, nchunks),
        in_specs=[
            pl.BlockSpec((1, _HALO, _CONV_DIM), lambda b, c: (b, jnp.maximum(c * hb - 1, 0), 0)),
            pl.BlockSpec((1, _CHUNK, _CONV_DIM), lambda b, c: (b, c, 0)),
            pl.BlockSpec((1, _HALO, _CONV_DIM), lambda b, c: (b, jnp.minimum((c + 1) * hb, nchunks * hb - 1), 0)),
            pl.BlockSpec((1, _CHUNK, _DT_W), lambda b, c: (b, c, 0)),
            const((_CONV_W, _CONV_DIM)), const((1, _CONV_DIM)), const((1, _DT_W)), const((1, _DT_W)),
            const((1, _SSD_INNER)),
        ],
        out_specs=[
            pl.BlockSpec((1, _CHUNK, _CONV_DIM), lambda b, c: (b, c, 0)),
            pl.BlockSpec((1, _CHUNK, _SSD_INNER), lambda b, c: (b, c, 0)),
        ],
        out_shape=[
            jax.ShapeDtypeStruct((bsz, seq, _CONV_DIM), _BF16),
            jax.ShapeDtypeStruct((bsz, seq, _SSD_INNER), _F32),
        ],
        scratch_shapes=[pltpu.VMEM((_CHUNK + 2 * _HALO, _CONV_DIM), _F32)] + _ssd_scratch(),
        compiler_params=_cparams("parallel", "arbitrary"),
        name="ssd_fwd",
    )(xbc3, xbc3, xbc3, dt3, conv_w, conv_b, dtb, alog, dskip)
    return act, yf


def _ssd_bwd(act, dt, dtb, alog, yf, z, norm_g, bsz, seq):
    nchunks = seq // _CHUNK
    dt3 = dt.reshape(bsz, seq, _DT_W)
    z3 = z.reshape(bsz, seq, _SSD_INNER)
    epair = _spread_matrix()
    const = lambda shape: pl.BlockSpec(shape, lambda b, c: (0,) * len(shape))
    blk = lambda w: pl.BlockSpec((1, _CHUNK, w), lambda b, c: (b, nchunks - 1 - c, 0))
    return pl.pallas_call(
        _ssd_bwd_body,
        grid=(bsz, nchunks),
        in_specs=[blk(_CONV_DIM), blk(_DT_W), const((1, _DT_W)), const((1, _DT_W)),
                  blk(_SSD_INNER), blk(_SSD_INNER), const((1, _SSD_INNER)), const(epair.shape)],
        out_specs=blk(_SSD_INNER),
        out_shape=jax.ShapeDtypeStruct((bsz, seq, _SSD_INNER), _BF16),
        scratch_shapes=_ssd_scratch() + [pltpu.VMEM((_CHUNK, _SSD_INNER), _F32)],
        compiler_params=_cparams("parallel", "arbitrary"),
        name="ssd_bwd",
    )(act, dt3, dtb, alog, yf, z3, norm_g, epair)


_QB = 128


def _attn_body(q_ref, kp_ref, kc_ref, kn_ref, vp_ref, vc_ref, vn_ref, o_ref, lse_ref, *, nblk):
    n = pl.program_id(2)
    ii = lax.broadcasted_iota(jnp.int32, (_QB, _QB), 0)
    jj = lax.broadcasted_iota(jnp.int32, (_QB, _QB), 1)
    dlt = jj - ii
    masks = (
        jnp.logical_and(dlt >= _QB - _ATT_RADIUS, n > 0),
        jnp.abs(dlt) <= _ATT_RADIUS,
        jnp.logical_and(dlt <= _ATT_RADIUS - _QB, n < nblk - 1),
    )
    lane_lo = lax.broadcasted_iota(jnp.int32, (1, _LANES), 1) < _ATT_HD
    nt = (((1,), (1,)), ((), ()))
    npair = _ATT_OUT // _LANES
    heads = [(p, u) for p in range(npair) for u in range(2)]
    sl = [slice(p * _LANES, (p + 1) * _LANES) for p in range(npair)]
    ks = [(kp_ref[0, 0, :, sl[p]], kc_ref[0, 0, :, sl[p]], kn_ref[0, 0, :, sl[p]]) for p in range(npair)]
    vs = [(vp_ref[0, 0, :, sl[p]], vc_ref[0, 0, :, sl[p]], vn_ref[0, 0, :, sl[p]]) for p in range(npair)]
    scores = []
    for p, u in heads:
        qp = q_ref[0, 0, :, sl[p]]
        hm = lane_lo if u == 0 else jnp.logical_not(lane_lo)
        qm = jnp.where(hm, qp, jnp.zeros_like(qp))
        scores.append([jnp.where(m, lax.dot_general(qm, kx, nt, preferred_element_type=_F32), _NEG_BIG)
                       for m, kx in zip(masks, ks[p])])
    mxs = [jnp.max(jnp.maximum(jnp.maximum(s[0], s[1]), s[2]), axis=-1, keepdims=True) for s in scores]
    probs = [[jnp.exp(sx - mx) for sx in s] for s, mx in zip(scores, mxs)]
    dens = [jnp.sum(pr[0] + pr[1] + pr[2], axis=-1, keepdims=True) for pr in probs]
    outs = []
    for (p, u), pr in zip(heads, probs):
        outs.append(sum(jnp.dot(px.astype(_BF16), vx, preferred_element_type=_F32) for px, vx in zip(pr, vs[p])))
    for p in range(npair):
        o0, o1 = outs[2 * p] / dens[2 * p], outs[2 * p + 1] / dens[2 * p + 1]
        l0 = jnp.broadcast_to(mxs[2 * p] + jnp.log(dens[2 * p]), (_QB, _LANES))
        l1 = jnp.broadcast_to(mxs[2 * p + 1] + jnp.log(dens[2 * p + 1]), (_QB, _LANES))
        o_ref[0, 0, :, sl[p]] = jnp.where(lane_lo, o0, o1)
        lse_ref[0, 0, :, sl[p]] = jnp.where(lane_lo, l0, l1)


def _attention_group(q, k, v, bsz, seq, dil):
    t = seq // dil
    nblk = t // _QB

    def spec(shift):
        return pl.BlockSpec((1, 1, _QB, _ATT_OUT), lambda b, r, n: (b, r, jnp.clip(n + shift, 0, nblk - 1), 0))

    return pl.pallas_call(
        functools.partial(_attn_body, nblk=nblk),
        grid=(bsz, dil, nblk),
        in_specs=[spec(0), spec(-1), spec(0), spec(1), spec(-1), spec(0), spec(1)],
        out_specs=[spec(0), spec(0)],
        out_shape=[jax.ShapeDtypeStruct((bsz, dil, t, _ATT_OUT), _F32)] * 2,
        compiler_params=_cparams("parallel", "parallel", "parallel"),
        name=f"attn_d{dil}",
    )(q, k, k, k, v, v, v)


def _finish_body(yn_ref, o0_ref, l0_ref, o1_ref, l1_ref, o2_ref, l2_ref, gate_ref, x_ref,
                 wa_ref, wb_ref, wo_ref, ng_ref, wq_ref, x1_ref, h2_ref, qp_ref, mix_ref):
    tm = x_ref.shape[0]
    npair = _ATT_OUT // _LANES

    def token_order(ref, slot, dil):
        if dil == 1:
            return [ref[0, 0, :, p * _LANES:(p + 1) * _LANES] for p in range(npair)]
        for p in range(npair):
            for r in range(dil):
                mix_ref[slot * npair + p, pl.ds(r, tm // dil, stride=dil), :] = ref[0, r, :, p * _LANES:(p + 1) * _LANES]
        return [mix_ref[slot * npair + p] for p in range(npair)]

    dils = [d for _, d in _ATT_GROUPS]
    o_parts = [token_order(r, 2 * g, dils[g]) for g, r in enumerate((o0_ref, o1_ref, o2_ref))]
    l_parts = [token_order(r, 2 * g + 1, dils[g]) for g, r in enumerate((l0_ref, l1_ref, l2_ref))]
    halves = []
    for p in range(npair):
        l0, l1, l2 = l_parts[0][p], l_parts[1][p], l_parts[2][p]
        lm = jnp.maximum(jnp.maximum(l0, l1), l2)
        e0, e1, e2 = jnp.exp(l0 - lm), jnp.exp(l1 - lm), jnp.exp(l2 - lm)
        den = e0 + e1 + e2
        halves.append((e0 / den) * o_parts[0][p] + (e1 / den) * o_parts[1][p] + (e2 / den) * o_parts[2][p])
    o = jnp.concatenate(halves, axis=1)
    a = jnp.dot(yn_ref[...], wa_ref[...], preferred_element_type=_F32)
    b = jnp.dot(o.astype(_BF16), wb_ref[...], preferred_element_type=_F32)
    ga = _sigmoid(gate_ref[:, 0:_D].astype(_F32))
    gb = _sigmoid(gate_ref[:, _D:2 * _D].astype(_F32))
    merged = (ga * a + gb * b).astype(_BF16)
    x1 = x_ref[...] + jnp.dot(merged, wo_ref[...], preferred_element_type=_F32)
    x1_ref[...] = x1
    y = x1 * lax.rsqrt(jnp.mean(x1 * x1, axis=-1, keepdims=True) + _EPS)
    h2 = (y * ng_ref[...]).astype(_BF16)
    h2_ref[...] = h2
    for hc in range(2 * _PEER_HEADS):
        sl = slice(hc * _PEER_KEYS, (hc + 1) * _PEER_KEYS)
        qp_ref[hc] = jnp.dot(h2, wq_ref[:, sl], preferred_element_type=_F32).astype(qp_ref.dtype)


def _finish(yn, attn, gates, x2d, wa, wb, wo, ng, wq, seq, tm):
    n = x2d.shape[0]
    nseq = seq // tm
    row = lambda w: pl.BlockSpec((tm, w), lambda i: (i, 0))
    const = lambda shape: pl.BlockSpec(shape, lambda i: (0, 0), pipeline_mode=pl.Buffered(1))
    cls = lambda dil: pl.BlockSpec((1, dil, tm // dil, _ATT_OUT), lambda i: (i // nseq, 0, i % nseq, 0))
    attn_specs = [cls(dil) for _, dil in _ATT_GROUPS for _ in range(2)]
    nq = 2 * _PEER_HEADS
    return pl.pallas_call(
        _finish_body,
        grid=(n // tm,),
        in_specs=[row(_SSD_INNER)] + attn_specs + [row(_GATE_W), row(_D),
                  const((_SSD_INNER, _D)), const((_ATT_OUT, _D)), const((_D, _D)), const((1, _D)),
                  const((_D, nq * _PEER_KEYS))],
        out_specs=[row(_D), row(_D), pl.BlockSpec((nq, tm, _PEER_KEYS), lambda i: (0, i, 0))],
        out_shape=[jax.ShapeDtypeStruct((n, _D), _F32), jax.ShapeDtypeStruct((n, _D), _BF16),
                   jax.ShapeDtypeStruct((nq, n, _PEER_KEYS), _BF16)],
        scratch_shapes=[pltpu.VMEM((2 * len(_ATT_GROUPS) * (_ATT_OUT // _LANES), tm, _LANES), _F32)],
        compiler_params=_cparams("parallel"),
        name="finish",
    )(yn, *attn, gates, x2d, wa, wb, wo, ng, wq)


_ROUTE_TB = 1024
_ROUTE_SUB = _ROUTE_TB // _LANES
_ROUTE_CHAINS = 4


def _sweep_top(ref, nrows, k, codes_ref=None):
    sub = _ROUTE_SUB
    neg = jnp.full((sub, _LANES), -jnp.inf, _F32)
    vals, outs = [], []
    prev = None
    for r in range(k):
        best = [None] * _ROUTE_CHAINS
        for row in range(nrows):
            v = ref[row * sub:(row + 1) * sub, :]
            if prev is not None:
                v = jnp.where(prev == row, neg, v)
                ref[row * sub:(row + 1) * sub, :] = v
            c = row % _ROUTE_CHAINS
            rid = jnp.full((sub, _LANES), row, jnp.int32)
            code = codes_ref[row * sub:(row + 1) * sub, :] if codes_ref is not None else None
            if best[c] is None:
                best[c] = (v, rid, code)
            else:
                bv, bk, bc = best[c]
                better = v > bv
                best[c] = (jnp.maximum(bv, v), jnp.where(better, rid, bk),
                           jnp.where(better, code, bc) if code is not None else None)
        acc = best[0]
        for c in range(1, min(_ROUTE_CHAINS, nrows)):
            av, ak, ac = acc
            bv, bk, bc = best[c]
            take = jnp.logical_or(bv > av, jnp.logical_and(bv == av, bk < ak))
            acc = (jnp.maximum(av, bv), jnp.where(take, bk, ak), jnp.where(take, bc, ac) if ac is not None else None)
        vals.append(acc[0])
        outs.append(acc[2] if codes_ref is not None else acc[1])
        prev = acc[1]
    return vals, outs


def _route_body(qp_ref, keys_ref, i_ref, j_ref, g_ref, s_ref, cand_ref, code_ref, is_ref, js_ref, gs_ref):
    nt = (((1,), (1,)), ((), ()))
    k = _PEER_TOPK
    sub = _ROUTE_SUB
    pairs = [(a, b) for a in range(k) for b in range(k // (a + 1))]

    def head(h, carry):
        hv, hi = [], []
        for c in range(2):
            for a in range(sub):
                tile = lax.dot_general(keys_ref[2 * h + c], qp_ref[2 * h + c, a * _LANES:(a + 1) * _LANES, :], nt,
                                       preferred_element_type=_F32)
                s_ref[c, pl.ds(a, _PEER_KEYS, stride=sub), :] = tile
            v, i = _sweep_top(s_ref.at[c], _PEER_KEYS, k)
            hv.append(v)
            hi.append(i)
        for p, (a, b) in enumerate(pairs):
            cand_ref[p * sub:(p + 1) * sub, :] = hv[0][a] + hv[1][b]
            code_ref[p * sub:(p + 1) * sub, :] = hi[0][a] * _PEER_KEYS + hi[1][b]
        fs, codes = _sweep_top(cand_ref, len(pairs), k, codes_ref=code_ref)
        e = [jnp.exp(f - fs[0]) for f in fs]
        den = e[0]
        for x in e[1:]:
            den = den + x
        for r in range(k):
            r0 = pl.multiple_of((h * k + r) * sub, sub)
            is_ref[pl.ds(r0, sub), :] = jnp.right_shift(codes[r], 7).astype(_F32)
            js_ref[pl.ds(r0, sub), :] = jnp.bitwise_and(codes[r], _PEER_KEYS - 1).astype(_F32)
            gs_ref[pl.ds(r0, sub), :] = e[r] / den
        return carry

    lax.fori_loop(0, _PEER_HEADS, head, 0)
    slots = _PEER_HEADS * k
    for src, dst in ((is_ref, i_ref), (js_ref, j_ref), (gs_ref, g_ref)):
        for a in range(sub):
            dst[a * _LANES:(a + 1) * _LANES, :] = src[pl.ds(a, slots, stride=sub), :].T


def _peer_route(qp, sub_keys16):
    nq, n, _ = qp.shape
    tq = _ROUTE_TB
    slots = _PEER_HEADS * _PEER_TOPK
    ncand = sum(_PEER_TOPK // (a + 1) for a in range(_PEER_TOPK))
    out = pl.BlockSpec((tq, slots), lambda i: (i, 0))
    return pl.pallas_call(
        _route_body,
        grid=(n // tq,),
        in_specs=[pl.BlockSpec((nq, tq, _PEER_KEYS), lambda i: (0, i, 0)),
                  pl.BlockSpec((nq, _PEER_KEYS, _PEER_KEYS), lambda i: (0, 0, 0))],
        out_specs=[out] * 3,
        out_shape=[jax.ShapeDtypeStruct((n, slots), _F32)] * 3,
        scratch_shapes=[pltpu.VMEM((2, _PEER_KEYS * _ROUTE_SUB, _LANES), _F32),
                        pltpu.VMEM((ncand * _ROUTE_SUB, _LANES), _F32),
                        pltpu.VMEM((ncand * _ROUTE_SUB, _LANES), jnp.int32)]
        + [pltpu.VMEM((slots * _ROUTE_SUB, _LANES), _F32)] * 3,
        compiler_params=_cparams("parallel"),
        name="peer_route",
    )(qp, sub_keys16)


_PEER_TM = 512
_PEER_HALF = _PEER_TM // 2
_PEER_EB = 1024
_PEER_CW = 256
_PEER_PITCH = _PEER_HALF + 8
_PEER_UNROLL = 32


def _peer_body(h_ref, i_ref, j_ref, g_ref, x1_ref, u_ref, v_ref, o_ref, gate_ref, w_ref):
    e = pl.program_id(1)
    slots = _PEER_HEADS * _PEER_TOPK
    nt = (((1,), (1,)), ((), ()))
    hi_mask = jnp.uint32(0xFFFF0000)

    @pl.when(e == 0)
    def _():
        o_ref[...] = x1_ref[...]
        sub = lax.broadcasted_iota(jnp.int32, (_PEER_KEYS, slots), 0).astype(_F32).astype(_BF16)
        one = jnp.ones((), _BF16)
        zero = jnp.zeros((), _BF16)

        def gate_bits(n):
            irow = i_ref[pl.ds(n, 1), :].astype(_BF16)
            jrow = j_ref[pl.ds(n, 1), :].astype(_BF16)
            grow = g_ref[pl.ds(n, 1), :].astype(_BF16)
            p_t = jnp.where(sub == irow, one, zero)
            q_t = jnp.where(sub == jrow, grow, zero)
            gm = lax.dot_general(p_t, q_t, nt, preferred_element_type=_F32)
            return pltpu.bitcast(gm, jnp.uint32)

        def pair(n, carry):
            lo = jnp.right_shift(gate_bits(n), jnp.uint32(16))
            hi = jnp.bitwise_and(gate_bits(n + _PEER_HALF), hi_mask)
            gate_ref[pl.ds(n, _PEER_KEYS, stride=_PEER_PITCH), :] = jnp.bitwise_or(lo, hi)
            return carry

        lax.fori_loop(0, _PEER_HALF, pair, 0, unroll=_PEER_UNROLL)

    rows_per_chunk = _PEER_CW // _PEER_KEYS
    h = h_ref[...]
    for c in range(_PEER_EB // _PEER_CW):
        hmat = lax.dot_general(h, u_ref[c * _PEER_CW:(c + 1) * _PEER_CW, :], nt, preferred_element_type=_F32)
        act = 0.5 * hmat * (1.0 + lax.erf(hmat * (1.0 / math.sqrt(2.0))))
        cols = []
        for r in range(rows_per_chunk):
            row = e * (_PEER_EB // _PEER_KEYS) + c * rows_per_chunk + r
            word = gate_ref[pl.ds(pl.multiple_of(row * _PEER_PITCH, 8), _PEER_HALF), :]
            lo = pltpu.bitcast(jnp.left_shift(word, jnp.uint32(16)), _F32)
            hi = pltpu.bitcast(jnp.bitwise_and(word, hi_mask), _F32)
            cols.append(jnp.concatenate([lo, hi], axis=0))
        gate = jnp.concatenate(cols, axis=1)
        w_ref[:, c * _PEER_CW:(c + 1) * _PEER_CW] = (gate * act).astype(_BF16)
    o_ref[...] += jnp.dot(w_ref[...], v_ref[...], preferred_element_type=_F32)


def _peer_dense(h2, ri, rj, rg, x1, u16, v16):
    n = h2.shape[0]
    tm, eb = _PEER_TM, _PEER_EB
    slots = _PEER_HEADS * _PEER_TOPK
    nexp = u16.shape[0]
    tok = lambda w: pl.BlockSpec((tm, w), lambda i, e: (i, 0))
    return pl.pallas_call(
        _peer_body,
        grid=(n // tm, nexp // eb),
        in_specs=[tok(_D), tok(slots), tok(slots), tok(slots), tok(_D),
                  pl.BlockSpec((eb, _D), lambda i, e: (e, 0)),
                  pl.BlockSpec((eb, _D), lambda i, e: (e, 0))],
        out_specs=tok(_D),
        out_shape=jax.ShapeDtypeStruct((n, _D), _F32),
        scratch_shapes=[pltpu.VMEM((_PEER_KEYS * _PEER_PITCH, _PEER_KEYS), jnp.uint32),
                        pltpu.VMEM((tm, eb), _BF16)],
        compiler_params=_cparams("parallel", "arbitrary"),
        name="peer_dense",
    )(h2, ri, rj, rg, x1, u16, v16)


def _layer(x, norm_mix_g, w_in, conv_w, conv_b, dt_bias, a_log, d_skip, ssd_norm_g, w_a,
           q_norm_g, k_norm_g, w_b, w_o, norm_ffn_g, w_query, sub_keys, expert_u, expert_v):
    bsz, seq, _ = x.shape
    n = bsz * seq
    x2d = x.reshape(n, _D)
    z_end = _SSD_INNER
    xbc_end = z_end + _CONV_DIM
    dt_end = xbc_end + 2 * _SSD_HEADS
    v_end = dt_end + _QKV_W
    w_perm = jnp.concatenate(
        [w_in[:, :xbc_end], w_in[:, dt_end:], w_in[:, xbc_end:dt_end],
         jnp.zeros((_D, _DT_W - 2 * _SSD_HEADS), w_in.dtype)], axis=1).astype(_BF16)
    pad_dt = lambda a: jnp.pad(a.astype(_F32).reshape(1, 2 * _SSD_HEADS), ((0, 0), (0, _DT_W - 2 * _SSD_HEADS)))

    z, xbc, qkv, gates, dt = _in_proj(x2d, norm_mix_g.astype(_F32)[None, :], w_perm, tm=512)
    qkv_groups = _qk_prep(qkv, q_norm_g, k_norm_g, bsz, seq, tm=512)
    act, yf = _ssd_fwd(xbc, dt, conv_w.astype(_F32), conv_b.astype(_F32)[None, :], pad_dt(dt_bias), pad_dt(a_log),
                       jnp.repeat(d_skip.astype(_F32), _SSD_HEAD_DIM)[None, :], bsz, seq)
    yn = _ssd_bwd(act, dt, pad_dt(dt_bias), pad_dt(a_log), yf, z, ssd_norm_g.astype(_F32)[None, :], bsz, seq)
    attn = []
    for gi, (window, dil) in enumerate(_ATT_GROUPS):
        assert window // (2 * dil) == _ATT_RADIUS
        attn.extend(_attention_group(*qkv_groups[3 * gi:3 * gi + 3], bsz, seq, dil))
    x1, h2, qp = _finish(yn.reshape(n, _SSD_INNER), attn, gates, x2d, w_a.astype(_BF16), w_b.astype(_BF16),
                         w_o.astype(_BF16), norm_ffn_g.astype(_F32)[None, :], w_query.astype(_BF16), seq, tm=512)
    keys16 = sub_keys.reshape(2 * _PEER_HEADS, _PEER_KEYS, sub_keys.shape[-1]).astype(_BF16)
    ri, rj, rg = _peer_route(qp, keys16)
    out = _peer_dense(h2, ri, rj, rg, x1, expert_u.astype(_BF16), expert_v.astype(_BF16))
    return out.reshape(bsz, seq, _D)


def kernel(x_prompt, x_sample, norm_mix_g, w_in, conv_w, conv_b, dt_bias, a_log, d_skip, ssd_norm_g, w_a,
           q_norm_g, k_norm_g, w_b, w_o, norm_ffn_g, w_query, sub_keys, expert_u, expert_v):
    depth = w_in.shape[0]
    nb = x_prompt.shape[0]
    x = jnp.concatenate([x_prompt, x_sample], axis=0)
    for l in range(depth):
        x = _layer(x, norm_mix_g[l], w_in[l], conv_w[l], conv_b[l], dt_bias[l], a_log[l], d_skip[l],
                   ssd_norm_g[l], w_a[l], q_norm_g[l], k_norm_g[l], w_b[l], w_o[l], norm_ffn_g[l],
                   w_query[l], sub_keys[l], expert_u[l], expert_v[l])
    return (x[:nb], x[nb:])
```

```python
import functools
import math

import jax
import jax.numpy as jnp
from jax import lax
from jax.experimental import pallas as pl
from jax.experimental.pallas import tpu as pltpu

_F32 = jnp.float32
_BF16 = jnp.bfloat16
_HIGHEST = lax.Precision.HIGHEST

_V7X_VMEM_BYTES = 64 * 1024 * 1024
_VMEM_LIMIT = _V7X_VMEM_BYTES - 8 * 1024 * 1024
_LANES = 128

_EPS = 1e-6
_D = 1024
_SSD_HEADS = 32
_SSD_HEAD_DIM = 64
_SSD_INNER = 2048
_SSD_GROUPS = 4
_SSD_STATE = 128
_SSD_BC = 512
_CONV_W = 5
_CONV_DIM = 3072
_CHUNK = 128
_HALO = 16
_ATT_GROUPS = ((128, 1), (512, 4), (2048, 16))
_ATT_WIDTH = 768
_ATT_OUT = 256
_ATT_HD = 64
_ATT_RADIUS = 64
_ROPE_DIMS = 16
_ROPE_THETA = 500000.0
_NEG_BIG = -1e30
_PEER_HEADS = 8
_PEER_KEYS = 128
_PEER_TOPK = 16
_QKV_W = 3 * _ATT_WIDTH
_GATE_W = 2 * _D
_DT_W = 128
_IN_COLS_PAD = _SSD_INNER + _CONV_DIM + _QKV_W + _GATE_W + _DT_W


def _cparams(*sem):
    return pltpu.CompilerParams(dimension_semantics=sem, vmem_limit_bytes=_VMEM_LIMIT)


def _sigmoid(x):
    return 1.0 / (1.0 + jnp.exp(-x))


_COL_CHUNK = 512


def _inproj_body(x_ref, g_ref, w_ref, z_ref, xbc_ref, qkv_ref, gate_ref, dt_ref):
    x = x_ref[...]
    y = x * lax.rsqrt(jnp.mean(x * x, axis=-1, keepdims=True) + _EPS)
    h = (y * g_ref[...]).astype(_BF16)
    col = 0
    for ref in (z_ref, xbc_ref, qkv_ref, gate_ref, dt_ref):
        width = ref.shape[-1]
        for c0 in range(0, width, _COL_CHUNK):
            cw = min(_COL_CHUNK, width - c0)
            acc = jnp.dot(h, w_ref[:, col + c0:col + c0 + cw], preferred_element_type=_F32)
            ref[:, c0:c0 + cw] = acc.astype(ref.dtype)
        col += width


def _in_proj(x2d, g, w_perm, tm):
    n = x2d.shape[0]
    widths = (_SSD_INNER, _CONV_DIM, _QKV_W, _GATE_W, _DT_W)
    dtypes = (_BF16, _BF16, _BF16, _BF16, _F32)
    return pl.pallas_call(
        _inproj_body,
        grid=(n // tm,),
        in_specs=[
            pl.BlockSpec((tm, _D), lambda i: (i, 0)),
            pl.BlockSpec((1, _D), lambda i: (0, 0)),
            pl.BlockSpec((_D, _IN_COLS_PAD), lambda i: (0, 0), pipeline_mode=pl.Buffered(1)),
        ],
        out_specs=[pl.BlockSpec((tm, w), lambda i: (i, 0)) for w in widths],
        out_shape=[jax.ShapeDtypeStruct((n, w), dt) for w, dt in zip(widths, dtypes)],
        compiler_params=_cparams("parallel"),
        name="in_proj",
    )(x2d, g, w_perm)


def _qkprep_body(q_ref, k_ref, v_ref, qg_ref, kg_ref, c_ref, s1_ref, s2_ref, bd_ref, *refs):
    outs, tmp_ref = refs[:-1], refs[-1]
    tm = q_ref.shape[0]
    cosv = c_ref[...]
    s1 = s1_ref[...]
    s2 = s2_ref[...]
    bd = bd_ref[...]
    cols_per_group = _ATT_OUT // _LANES
    for ti, (src, g_ref, scale) in enumerate(((q_ref, qg_ref, _ATT_HD ** -0.5), (k_ref, kg_ref, 1.0), (v_ref, None, 1.0))):
        for p in range(_ATT_WIDTH // _LANES):
            x = src[:, p * _LANES:(p + 1) * _LANES].astype(_F32)
            if g_ref is not None:
                ss = jnp.dot(_split3(x * x), bd, preferred_element_type=_F32)
                y = x * lax.rsqrt(ss * (1.0 / _ATT_HD) + _EPS) * g_ref[...]
                x = y * cosv + pltpu.roll(y, 8, 1) * s1 + pltpu.roll(y, _LANES - 8, 1) * s2
                if scale != 1.0:
                    x = x * scale
            gi, pl_ = divmod(p, cols_per_group)
            dil = _ATT_GROUPS[gi][1]
            dst = outs[3 * gi + ti]
            lanes = slice(pl_ * _LANES, (pl_ + 1) * _LANES)
            if dil == 1:
                dst[0, 0, :, lanes] = x.astype(dst.dtype)
            else:
                tmp_ref[...] = x
                for r in range(dil):
                    dst[0, r, :, lanes] = tmp_ref[pl.ds(r, tm // dil, stride=dil), :].astype(dst.dtype)


def _rope_tables(seq):
    half = _ROPE_DIMS // 2
    inv = _ROPE_THETA ** (-jnp.arange(half, dtype=_F32) * 2.0 / _ROPE_DIMS)
    ang = jnp.arange(seq, dtype=_F32)[:, None] * inv[None, :]
    cos, sin = jnp.cos(ang), jnp.sin(ang)
    ones = jnp.ones((seq, _ATT_HD - _ROPE_DIMS), _F32)
    zeros = jnp.zeros((seq, _ATT_HD - _ROPE_DIMS), _F32)
    zh = jnp.zeros((seq, half), _F32)
    c = jnp.concatenate([cos, cos, ones], axis=1)
    s1 = jnp.concatenate([zh, sin, zeros], axis=1)
    s2 = jnp.concatenate([-sin, zh, zeros], axis=1)
    rep = _LANES // _ATT_HD
    return tuple(jnp.tile(t, (1, rep)) for t in (c, s1, s2))


def _qk_prep(qkv, q_g, k_g, bsz, seq, tm):
    c, s1, s2 = _rope_tables(seq)
    nseq = seq // tm
    lane = jnp.arange(_LANES)
    bd = jnp.tile((lane[:, None] // _ATT_HD == lane[None, :] // _ATT_HD).astype(_BF16), (3, 1))
    rep = _LANES // _ATT_HD
    qg = jnp.tile(q_g.astype(_F32), rep)[None, :]
    kg = jnp.tile(k_g.astype(_F32), rep)[None, :]
    tab = pl.BlockSpec((tm, _LANES), lambda b, i: (i, 0))
    const = lambda shape: pl.BlockSpec(shape, lambda b, i: (0, 0))
    col = lambda cblk: pl.BlockSpec((tm, _ATT_WIDTH), lambda b, i: (b * nseq + i, cblk))
    out_specs, out_shapes = [], []
    for _, dil in _ATT_GROUPS:
        for _ in range(3):
            out_specs.append(pl.BlockSpec((1, dil, tm // dil, _ATT_OUT), lambda b, i: (b, 0, i, 0)))
            out_shapes.append(jax.ShapeDtypeStruct((bsz, dil, seq // dil, _ATT_OUT), _BF16))
    return pl.pallas_call(
        _qkprep_body,
        grid=(bsz, nseq),
        in_specs=[col(0), col(1), col(2), const((1, _LANES)), const((1, _LANES)), tab, tab, tab,
                  const((3 * _LANES, _LANES))],
        out_specs=out_specs,
        out_shape=out_shapes,
        scratch_shapes=[pltpu.VMEM((tm, _LANES), _F32)],
        compiler_params=_cparams("parallel", "parallel"),
        name="qk_prep",
    )(qkv, qkv, qkv, qg, kg, c, s1, s2, bd)


def _pair_bcast(a, lane_lo, h):
    rows = a.shape[0]
    lo = jnp.broadcast_to(a[:, h:h + 1], (rows, _LANES))
    hi = jnp.broadcast_to(a[:, h + 1:h + 2], (rows, _LANES))
    return jnp.where(lane_lo, lo, hi)


def _split3(x):
    hi = x.astype(_BF16)
    r1 = x - hi.astype(_F32)
    mid = r1.astype(_BF16)
    lo = (r1 - mid.astype(_F32)).astype(_BF16)
    return jnp.concatenate([hi, mid, lo], axis=1)


def _ssd_chunk(act_ref, dt, da, rev, epair_ref, state_ref, xsc_ref, emit):
    L = _CHUNK
    row = lax.broadcasted_iota(jnp.int32, (L, L), 0)
    col = lax.broadcasted_iota(jnp.int32, (L, L), 1)
    keep = (row <= col) if rev else (row >= col)
    keep_t = (row >= col) if rev else (row <= col)
    cs = jnp.dot(keep.astype(_F32), da, precision=_HIGHEST, preferred_element_type=_F32)
    cs_t = jnp.dot(da.T, keep_t.astype(_F32), precision=_HIGHEST, preferred_element_type=_F32)
    dt_t = dt.T
    cs_end = cs[0:1, :] if rev else cs[L - 1:L, :]
    cdec = jnp.exp(cs_end)
    ecs = jnp.exp(cs)
    wst = jnp.exp(cs_end - cs) * dt
    lane_lo = lax.broadcasted_iota(jnp.int32, (1, _LANES), 1) < _SSD_HEAD_DIM
    if epair_ref is not None:
        ecs_all = jnp.dot(_split3(ecs), epair_ref[...], preferred_element_type=_F32)
        wst_all = jnp.dot(_split3(wst), epair_ref[...], preferred_element_type=_F32)
        xsc_ref[...] = (act_ref[:, 0:_SSD_INNER] * wst_all).astype(_BF16)
    heads_per_group = _SSD_HEADS // _SSD_GROUPS
    pairs_per_group = heads_per_group // 2
    gw = heads_per_group * _SSD_HEAD_DIM
    for g in range(_SSD_GROUPS):
        bm = act_ref[:, _SSD_INNER + g * _SSD_STATE:_SSD_INNER + (g + 1) * _SSD_STATE]
        cm = act_ref[:, _SSD_INNER + _SSD_BC + g * _SSD_STATE:_SSD_INNER + _SSD_BC + (g + 1) * _SSD_STATE]
        bm16 = bm.astype(_BF16)
        cm16 = cm.astype(_BF16)
        cb = lax.dot_general(cm16, bm16, (((1,), (1,)), ((), ())), preferred_element_type=_F32)
        y_off = jnp.dot(cm16, state_ref[g].astype(_BF16), preferred_element_type=_F32)
        cdec_cols = []
        for jj in range(pairs_per_group):
            j = g * pairs_per_group + jj
            sl = slice(j * _LANES, (j + 1) * _LANES)
            xs = act_ref[:, sl]
            xs16 = xs.astype(_BF16)
            res = []
            for u in range(2):
                h = 2 * j + u
                diff = jnp.broadcast_to(cs[:, h:h + 1], (L, L)) - cs_t[h:h + 1, :]
                decay = jnp.exp(jnp.where(keep, diff, _NEG_BIG))
                m = (cb * decay * dt_t[h:h + 1, :]).astype(_BF16)
                res.append(jnp.dot(m, xs16, preferred_element_type=_F32))
            y = jnp.where(lane_lo, res[0], res[1])
            if epair_ref is not None:
                ec = ecs_all[:, sl]
            else:
                ec = _pair_bcast(ecs, lane_lo, 2 * j)
                xsc_ref[:, sl] = (xs * _pair_bcast(wst, lane_lo, 2 * j)).astype(_BF16)
            emit(j, y + y_off[:, jj * _LANES:(jj + 1) * _LANES] * ec)
            cdec_cols.append(_pair_bcast(cdec, lane_lo, 2 * j))
        new = jnp.dot(bm.T.astype(_BF16), xsc_ref[:, g * gw:(g + 1) * gw], preferred_element_type=_F32)
        state_ref[g] = state_ref[g] * jnp.concatenate(cdec_cols, axis=1) + new


def _softplus(x):
    return jnp.maximum(x, 0.0) + jnp.log(1.0 + jnp.exp(-jnp.abs(x)))


def _ssd_fwd_body(xp_ref, xc_ref, xn_ref, dt_ref, cw_ref, cb_ref, dtb_ref, alog_ref, dskip_ref,
                  act_out_ref, y_ref, win_ref, act_ref, state_ref, xsc_ref, *, nchunks):
    c = pl.program_id(1)

    @pl.when(c == 0)
    def _():
        state_ref[...] = jnp.zeros_like(state_ref)

    L = _CHUNK
    prev_ok = (c > 0).astype(_F32)
    next_ok = (c < nchunks - 1).astype(_F32)
    win_ref[0:_HALO, :] = xp_ref[0].astype(_F32) * prev_ok
    win_ref[_HALO:_HALO + L, :] = xc_ref[0].astype(_F32)
    win_ref[_HALO + L:2 * _HALO + L, :] = xn_ref[0].astype(_F32) * next_ok
    pad = (_CONV_W - 1) // 2
    cw = cw_ref[...]
    for c0 in range(0, _CONV_DIM, _COL_CHUNK):
        sl = slice(c0, c0 + _COL_CHUNK)
        acc = jnp.broadcast_to(cb_ref[:, sl], (L, _COL_CHUNK))
        for k in range(_CONV_W):
            acc = acc + win_ref[_HALO - pad + k:_HALO - pad + k + L, sl] * cw[k:k + 1, sl]
        a = acc * _sigmoid(acc)
        act_ref[:, sl] = a
        act_out_ref[0, :, sl] = a.astype(act_out_ref.dtype)
    dt = _softplus(dt_ref[0] + dtb_ref[...])
    da = dt * (-jnp.exp(alog_ref[...]))

    def emit(j, y):
        sl = slice(j * _LANES, (j + 1) * _LANES)
        y_ref[0, :, sl] = y + dskip_ref[:, sl] * act_ref[:, sl]

    _ssd_chunk(act_ref, dt, da, False, None, state_ref, xsc_ref, emit)


def _ssd_bwd_body(act_in_ref, dt_ref, dtb_ref, alog_ref, yf_ref, z_ref, ng_ref, epair_ref,
                  yn_ref, act_ref, state_ref, xsc_ref, ytot_ref):
    c = pl.program_id(1)

    @pl.when(c == 0)
    def _():
        state_ref[...] = jnp.zeros_like(state_ref)

    act_ref[...] = act_in_ref[0].astype(_F32)
    shift = _LANES - _SSD_HEADS
    dt = _softplus(pltpu.roll(dt_ref[0], shift, 1) + pltpu.roll(dtb_ref[...], shift, 1))
    da = dt * (-jnp.exp(pltpu.roll(alog_ref[...], shift, 1)))

    def emit(j, y):
        sl = slice(j * _LANES, (j + 1) * _LANES)
        z = z_ref[0, :, sl].astype(_F32)
        ytot_ref[:, sl] = (y + yf_ref[0, :, sl]) * (z * _sigmoid(z))

    _ssd_chunk(act_ref, dt, da, True, epair_ref, state_ref, xsc_ref, emit)
    gw = _SSD_INNER // _SSD_GROUPS
    for g in range(_SSD_GROUPS):
        t = ytot_ref[:, g * gw:(g + 1) * gw]
        r = lax.rsqrt(jnp.mean(t * t, axis=-1, keepdims=True) + _EPS)
        yn_ref[0, :, g * gw:(g + 1) * gw] = (t * r * ng_ref[:, g * gw:(g + 1) * gw]).astype(yn_ref.dtype)


def _spread_matrix():
    r = jnp.arange(3 * _LANES)[:, None] % _LANES
    return (r == jnp.arange(_SSD_INNER)[None, :] // _SSD_HEAD_DIM).astype(_BF16)


def _ssd_scratch():
    return [
        pltpu.VMEM((_CHUNK, _CONV_DIM), _F32),
        pltpu.VMEM((_SSD_GROUPS, _SSD_STATE, _SSD_INNER // _SSD_GROUPS), _F32),
        pltpu.VMEM((_CHUNK, _SSD_INNER), _BF16),
    ]


def _ssd_fwd(xbc, dt, conv_w, conv_b, dtb, alog, dskip, bsz, seq):
    nchunks = seq // _CHUNK
    hb = _CHUNK // _HALO
    xbc3 = xbc.reshape(bsz, seq, _CONV_DIM)
    dt3 = dt.reshape(bsz, seq, _DT_W)
    const = lambda shape: pl.BlockSpec(shape, lambda b, c: (0,) * len(shape))
    act, yf = pl.pallas_call(
        functools.partial(_ssd_fwd_body, nchunks=nchunks),
        grid=(bsz, nchunks),
        in_specs=[
            pl.BlockSpec((1, _HALO, _CONV_DIM), lambda b, c: (b, jnp.maximum(c * hb - 1, 0), 0)),
            pl.BlockSpec((1, _CHUNK, _CONV_DIM), lambda b, c: (b, c, 0)),
            pl.BlockSpec((1, _HALO, _CONV_DIM), lambda b, c: (b, jnp.minimum((c + 1) * hb, nchunks * hb - 1), 0)),
            pl.BlockSpec((1, _CHUNK, _DT_W), lambda b, c: (b, c, 0)),
            const((_CONV_W, _CONV_DIM)), const((1, _CONV_DIM)), const((1, _DT_W)), const((1, _DT_W)),
            const((1, _SSD_INNER)),
        ],
        out_specs=[
            pl.BlockSpec((1, _CHUNK, _CONV_DIM), lambda b, c: (b, c, 0)),
            pl.BlockSpec((1, _CHUNK, _SSD_INNER), lambda b, c: (b, c, 0)),
        ],
        out_shape=[
            jax.ShapeDtypeStruct((bsz, seq, _CONV_DIM), _BF16),
            jax.ShapeDtypeStruct((bsz, seq, _SSD_INNER), _F32),
        ],
        scratch_shapes=[pltpu.VMEM((_CHUNK + 2 * _HALO, _CONV_DIM), _F32)] + _ssd_scratch(),
        compiler_params=_cparams("parallel", "arbitrary"),
        name="ssd_fwd",
    )(xbc3, xbc3, xbc3, dt3, conv_w, conv_b, dtb, alog, dskip)
    return act, yf


def _ssd_bwd(act, dt, dtb, alog, yf, z, norm_g, bsz, seq):
    nchunks = seq // _CHUNK
    dt3 = dt.reshape(bsz, seq, _DT_W)
    z3 = z.reshape(bsz, seq, _SSD_INNER)
    epair = _spread_matrix()
    const = lambda shape: pl.BlockSpec(shape, lambda b, c: (0,) * len(shape))
    blk = lambda w: pl.BlockSpec((1, _CHUNK, w), lambda b, c: (b, nchunks - 1 - c, 0))
    return pl.pallas_call(
        _ssd_bwd_body,
        grid=(bsz, nchunks),
        in_specs=[blk(_CONV_DIM), blk(_DT_W), const((1, _DT_W)), const((1, _DT_W)),
                  blk(_SSD_INNER), blk(_SSD_INNER), const((1, _SSD_INNER)), const(epair.shape)],
        out_specs=blk(_SSD_INNER),
        out_shape=jax.ShapeDtypeStruct((bsz, seq, _SSD_INNER), _BF16),
        scratch_shapes=_ssd_scratch() + [pltpu.VMEM((_CHUNK, _SSD_INNER), _F32)],
        compiler_params=_cparams("parallel", "arbitrary"),
        name="ssd_bwd",
    )(act, dt3, dtb, alog, yf, z3, norm_g, epair)


_QB = 128


_QSUB = 2


def _attn_body(q_ref, kp_ref, kc_ref, kn_ref, vp_ref, vc_ref, vn_ref, o_ref, lse_ref, *, nstep):
    n = pl.program_id(2)
    ii = lax.broadcasted_iota(jnp.int32, (_QB, _QB), 0)
    jj = lax.broadcasted_iota(jnp.int32, (_QB, _QB), 1)
    dlt = jj - ii
    band = (dlt >= _QB - _ATT_RADIUS, jnp.abs(dlt) <= _ATT_RADIUS, dlt <= _ATT_RADIUS - _QB)
    lane_lo = lax.broadcasted_iota(jnp.int32, (1, _LANES), 1) < _ATT_HD
    nt = (((1,), (1,)), ((), ()))
    npair = _ATT_OUT // _LANES
    sl = [slice(p * _LANES, (p + 1) * _LANES) for p in range(npair)]

    def blocks(pr, cr, nr, p):
        return ([pr[0, 0, :, sl[p]]] + [cr[0, 0, s * _QB:(s + 1) * _QB, sl[p]] for s in range(_QSUB)]
                + [nr[0, 0, :, sl[p]]])

    ks = [blocks(kp_ref, kc_ref, kn_ref, p) for p in range(npair)]
    vs = [blocks(vp_ref, vc_ref, vn_ref, p) for p in range(npair)]
    chains = [(s, p, u) for s in range(_QSUB) for p in range(npair) for u in range(2)]
    scores = []
    for s, p, u in chains:
        qp = q_ref[0, 0, s * _QB:(s + 1) * _QB, sl[p]]
        hm = lane_lo if u == 0 else jnp.logical_not(lane_lo)
        qm = jnp.where(hm, qp, jnp.zeros_like(qp))
        masks = [band[0], band[1], band[2]]
        if s == 0:
            masks[0] = jnp.logical_and(band[0], n > 0)
        if s == _QSUB - 1:
            masks[2] = jnp.logical_and(band[2], n < nstep - 1)
        scores.append([jnp.where(m, lax.dot_general(qm, kx, nt, preferred_element_type=_F32), _NEG_BIG)
                       for m, kx in zip(masks, ks[p][s:s + 3])])
    mxs = [jnp.max(jnp.maximum(jnp.maximum(sc[0], sc[1]), sc[2]), axis=-1, keepdims=True) for sc in scores]
    probs = [[jnp.exp(sx - mx) for sx in sc] for sc, mx in zip(scores, mxs)]
    dens = [jnp.sum(pr[0] + pr[1] + pr[2], axis=-1, keepdims=True) for pr in probs]
    outs = []
    for (s, p, u), pr in zip(chains, probs):
        outs.append(sum(jnp.dot(px.astype(_BF16), vx, preferred_element_type=_F32)
                        for px, vx in zip(pr, vs[p][s:s + 3])))
    for c in range(0, len(chains), 2):
        s, p, _ = chains[c]
        rows = slice(s * _QB, (s + 1) * _QB)
        o0, o1 = outs[c] / dens[c], outs[c + 1] / dens[c + 1]
        l0 = jnp.broadcast_to(mxs[c] + jnp.log(dens[c]), (_QB, _LANES))
        l1 = jnp.broadcast_to(mxs[c + 1] + jnp.log(dens[c + 1]), (_QB, _LANES))
        o_ref[0, 0, rows, sl[p]] = jnp.where(lane_lo, o0, o1)
        lse_ref[0, 0, rows, sl[p]] = jnp.where(lane_lo, l0, l1)


def _attention_group(q, k, v, bsz, seq, dil):
    t = seq // dil
    nstep = t // (_QSUB * _QB)
    nblk = t // _QB
    own = pl.BlockSpec((1, 1, _QSUB * _QB, _ATT_OUT), lambda b, r, n: (b, r, n, 0))
    before = pl.BlockSpec((1, 1, _QB, _ATT_OUT), lambda b, r, n: (b, r, jnp.maximum(_QSUB * n - 1, 0), 0))
    after = pl.BlockSpec((1, 1, _QB, _ATT_OUT), lambda b, r, n: (b, r, jnp.minimum(_QSUB * (n + 1), nblk - 1), 0))
    return pl.pallas_call(
        functools.partial(_attn_body, nstep=nstep),
        grid=(bsz, dil, nstep),
        in_specs=[own, before, own, after, before, own, after],
        out_specs=[own, own],
        out_shape=[jax.ShapeDtypeStruct((bsz, dil, t, _ATT_OUT), _F32)] * 2,
        compiler_params=_cparams("parallel", "parallel", "parallel"),
        name=f"attn_d{dil}",
    )(q, k, k, k, v, v, v)


def _finish_body(yn_ref, o0_ref, l0_ref, o1_ref, l1_ref, o2_ref, l2_ref, gate_ref, x_ref,
                 wa_ref, wb_ref, wo_ref, ng_ref, wq_ref, x1_ref, h2_ref, qp_ref, mix_ref):
    tm = x_ref.shape[0]
    npair = _ATT_OUT // _LANES

    def token_order(ref, slot, dil):
        if dil == 1:
            return [ref[0, 0, :, p * _LANES:(p + 1) * _LANES] for p in range(npair)]
        for p in range(npair):
            for r in range(dil):
                mix_ref[slot * npair + p, pl.ds(r, tm // dil, stride=dil), :] = ref[0, r, :, p * _LANES:(p + 1) * _LANES]
        return [mix_ref[slot * npair + p] for p in range(npair)]

    dils = [d for _, d in _ATT_GROUPS]
    o_parts = [token_order(r, 2 * g, dils[g]) for g, r in enumerate((o0_ref, o1_ref, o2_ref))]
    l_parts = [token_order(r, 2 * g + 1, dils[g]) for g, r in enumerate((l0_ref, l1_ref, l2_ref))]
    halves = []
    for p in range(npair):
        l0, l1, l2 = l_parts[0][p], l_parts[1][p], l_parts[2][p]
        lm = jnp.maximum(jnp.maximum(l0, l1), l2)
        e0, e1, e2 = jnp.exp(l0 - lm), jnp.exp(l1 - lm), jnp.exp(l2 - lm)
        den = e0 + e1 + e2
        halves.append((e0 / den) * o_parts[0][p] + (e1 / den) * o_parts[1][p] + (e2 / den) * o_parts[2][p])
    o = jnp.concatenate(halves, axis=1)
    a = jnp.dot(yn_ref[...], wa_ref[...], preferred_element_type=_F32)
    b = jnp.dot(o.astype(_BF16), wb_ref[...], preferred_element_type=_F32)
    ga = _sigmoid(gate_ref[:, 0:_D].astype(_F32))
    gb = _sigmoid(gate_ref[:, _D:2 * _D].astype(_F32))
    merged = (ga * a + gb * b).astype(_BF16)
    x1 = x_ref[...] + jnp.dot(merged, wo_ref[...], preferred_element_type=_F32)
    x1_ref[...] = x1
    y = x1 * lax.rsqrt(jnp.mean(x1 * x1, axis=-1, keepdims=True) + _EPS)
    h2 = (y * ng_ref[...]).astype(_BF16)
    h2_ref[...] = h2
    for hc in range(2 * _PEER_HEADS):
        sl = slice(hc * _PEER_KEYS, (hc + 1) * _PEER_KEYS)
        qp_ref[hc] = jnp.dot(h2, wq_ref[:, sl], preferred_element_type=_F32).astype(qp_ref.dtype)


def _finish(yn, attn, gates, x2d, wa, wb, wo, ng, wq, seq, tm):
    n = x2d.shape[0]
    nseq = seq // tm
    row = lambda w: pl.BlockSpec((tm, w), lambda i: (i, 0))
    const = lambda shape: pl.BlockSpec(shape, lambda i: (0, 0), pipeline_mode=pl.Buffered(1))
    cls = lambda dil: pl.BlockSpec((1, dil, tm // dil, _ATT_OUT), lambda i: (i // nseq, 0, i % nseq, 0))
    attn_specs = [cls(dil) for _, dil in _ATT_GROUPS for _ in range(2)]
    nq = 2 * _PEER_HEADS
    return pl.pallas_call(
        _finish_body,
        grid=(n // tm,),
        in_specs=[row(_SSD_INNER)] + attn_specs + [row(_GATE_W), row(_D),
                  const((_SSD_INNER, _D)), const((_ATT_OUT, _D)), const((_D, _D)), const((1, _D)),
                  const((_D, nq * _PEER_KEYS))],
        out_specs=[row(_D), row(_D), pl.BlockSpec((nq, tm, _PEER_KEYS), lambda i: (0, i, 0))],
        out_shape=[jax.ShapeDtypeStruct((n, _D), _F32), jax.ShapeDtypeStruct((n, _D), _BF16),
                   jax.ShapeDtypeStruct((nq, n, _PEER_KEYS), _BF16)],
        scratch_shapes=[pltpu.VMEM((2 * len(_ATT_GROUPS) * (_ATT_OUT // _LANES), tm, _LANES), _F32)],
        compiler_params=_cparams("parallel"),
        name="finish",
    )(yn, *attn, gates, x2d, wa, wb, wo, ng, wq)


_ROUTE_TB = 1024
_ROUTE_SUB = _ROUTE_TB // _LANES
_ROUTE_CHAINS = 4


def _sweep_top(ref, nrows, k, codes_ref=None):
    sub = _ROUTE_SUB
    neg = jnp.full((sub, _LANES), -jnp.inf, _F32)
    vals, outs = [], []
    prev = None
    for r in range(k):
        best = [None] * _ROUTE_CHAINS
        for row in range(nrows):
            v = ref[row * sub:(row + 1) * sub, :]
            if prev is not None:
                v = jnp.where(prev == row, neg, v)
                ref[row * sub:(row + 1) * sub, :] = v
            c = row % _ROUTE_CHAINS
            rid = jnp.full((sub, _LANES), row, jnp.int32)
            code = codes_ref[row * sub:(row + 1) * sub, :] if codes_ref is not None else None
            if best[c] is None:
                best[c] = (v, rid, code)
            else:
                bv, bk, bc = best[c]
                better = v > bv
                best[c] = (jnp.maximum(bv, v), jnp.where(better, rid, bk),
                           jnp.where(better, code, bc) if code is not None else None)
        acc = best[0]
        for c in range(1, min(_ROUTE_CHAINS, nrows)):
            av, ak, ac = acc
            bv, bk, bc = best[c]
            take = jnp.logical_or(bv > av, jnp.logical_and(bv == av, bk < ak))
            acc = (jnp.maximum(av, bv), jnp.where(take, bk, ak), jnp.where(take, bc, ac) if ac is not None else None)
        vals.append(acc[0])
        outs.append(acc[2] if codes_ref is not None else acc[1])
        prev = acc[1]
    return vals, outs


def _route_body(qp_ref, keys_ref, i_ref, j_ref, g_ref, s_ref, cand_ref, code_ref, is_ref, js_ref, gs_ref):
    nt = (((1,), (1,)), ((), ()))
    k = _PEER_TOPK
    sub = _ROUTE_SUB
    pairs = [(a, b) for a in range(k) for b in range(k // (a + 1))]

    def head(h, carry):
        hv, hi = [], []
        for c in range(2):
            for a in range(sub):
                tile = lax.dot_general(keys_ref[2 * h + c], qp_ref[2 * h + c, a * _LANES:(a + 1) * _LANES, :], nt,
                                       preferred_element_type=_F32)
                s_ref[c, pl.ds(a, _PEER_KEYS, stride=sub), :] = tile
            v, i = _sweep_top(s_ref.at[c], _PEER_KEYS, k)
            hv.append(v)
            hi.append(i)
        for p, (a, b) in enumerate(pairs):
            cand_ref[p * sub:(p + 1) * sub, :] = hv[0][a] + hv[1][b]
            code_ref[p * sub:(p + 1) * sub, :] = hi[0][a] * _PEER_KEYS + hi[1][b]
        fs, codes = _sweep_top(cand_ref, len(pairs), k, codes_ref=code_ref)
        e = [jnp.exp(f - fs[0]) for f in fs]
        den = e[0]
        for x in e[1:]:
            den = den + x
        for r in range(k):
            r0 = pl.multiple_of((h * k + r) * sub, sub)
            is_ref[pl.ds(r0, sub), :] = jnp.right_shift(codes[r], 7).astype(_F32)
            js_ref[pl.ds(r0, sub), :] = jnp.bitwise_and(codes[r], _PEER_KEYS - 1).astype(_F32)
            gs_ref[pl.ds(r0, sub), :] = e[r] / den
        return carry

    lax.fori_loop(0, _PEER_HEADS, head, 0)
    slots = _PEER_HEADS * k
    for src, dst in ((is_ref, i_ref), (js_ref, j_ref), (gs_ref, g_ref)):
        for a in range(sub):
            dst[a * _LANES:(a + 1) * _LANES, :] = src[pl.ds(a, slots, stride=sub), :].T


def _peer_route(qp, sub_keys16):
    nq, n, _ = qp.shape
    tq = _ROUTE_TB
    slots = _PEER_HEADS * _PEER_TOPK
    ncand = sum(_PEER_TOPK // (a + 1) for a in range(_PEER_TOPK))
    out = pl.BlockSpec((tq, slots), lambda i: (i, 0))
    return pl.pallas_call(
        _route_body,
        grid=(n // tq,),
        in_specs=[pl.BlockSpec((nq, tq, _PEER_KEYS), lambda i: (0, i, 0)),
                  pl.BlockSpec((nq, _PEER_KEYS, _PEER_KEYS), lambda i: (0, 0, 0))],
        out_specs=[out] * 3,
        out_shape=[jax.ShapeDtypeStruct((n, slots), _F32)] * 3,
        scratch_shapes=[pltpu.VMEM((2, _PEER_KEYS * _ROUTE_SUB, _LANES), _F32),
                        pltpu.VMEM((ncand * _ROUTE_SUB, _LANES), _F32),
                        pltpu.VMEM((ncand * _ROUTE_SUB, _LANES), jnp.int32)]
        + [pltpu.VMEM((slots * _ROUTE_SUB, _LANES), _F32)] * 3,
        compiler_params=_cparams("parallel"),
        name="peer_route",
    )(qp, sub_keys16)


_PEER_TM = 512
_PEER_HALF = _PEER_TM // 2
_PEER_EB = 2048
_PEER_CW = 256
_PEER_PITCH = _PEER_HALF + 8
_PEER_UNROLL = 32


def _peer_body(h_ref, i_ref, j_ref, g_ref, x1_ref, u_ref, v_ref, o_ref, gate_ref, w_ref):
    e = pl.program_id(1)
    slots = _PEER_HEADS * _PEER_TOPK
    nt = (((1,), (1,)), ((), ()))
    hi_mask = jnp.uint32(0xFFFF0000)

    @pl.when(e == 0)
    def _():
        o_ref[...] = x1_ref[...]
        sub = lax.broadcasted_iota(jnp.int32, (_PEER_KEYS, slots), 0).astype(_F32).astype(_BF16)
        one = jnp.ones((), _BF16)
        zero = jnp.zeros((), _BF16)

        def gate_bits(n):
            irow = i_ref[pl.ds(n, 1), :].astype(_BF16)
            jrow = j_ref[pl.ds(n, 1), :].astype(_BF16)
            grow = g_ref[pl.ds(n, 1), :].astype(_BF16)
            p_t = jnp.where(sub == irow, one, zero)
            q_t = jnp.where(sub == jrow, grow, zero)
            gm = lax.dot_general(p_t, q_t, nt, preferred_element_type=_F32)
            return pltpu.bitcast(gm, jnp.uint32)

        def pair(n, carry):
            lo = jnp.right_shift(gate_bits(n), jnp.uint32(16))
            hi = jnp.bitwise_and(gate_bits(n + _PEER_HALF), hi_mask)
            gate_ref[pl.ds(n, _PEER_KEYS, stride=_PEER_PITCH), :] = jnp.bitwise_or(lo, hi)
            return carry

        lax.fori_loop(0, _PEER_HALF, pair, 0, unroll=_PEER_UNROLL)

    rows_per_chunk = _PEER_CW // _PEER_KEYS
    h = h_ref[...]
    for c in range(_PEER_EB // _PEER_CW):
        hmat = lax.dot_general(h, u_ref[c * _PEER_CW:(c + 1) * _PEER_CW, :], nt, preferred_element_type=_F32)
        act = 0.5 * hmat * (1.0 + lax.erf(hmat * (1.0 / math.sqrt(2.0))))
        cols = []
        for r in range(rows_per_chunk):
            row = e * (_PEER_EB // _PEER_KEYS) + c * rows_per_chunk + r
            word = gate_ref[pl.ds(pl.multiple_of(row * _PEER_PITCH, 8), _PEER_HALF), :]
            lo = pltpu.bitcast(jnp.left_shift(word, jnp.uint32(16)), _F32)
            hi = pltpu.bitcast(jnp.bitwise_and(word, hi_mask), _F32)
            cols.append(jnp.concatenate([lo, hi], axis=0))
        gate = jnp.concatenate(cols, axis=1)
        w_ref[:, c * _PEER_CW:(c + 1) * _PEER_CW] = (gate * act).astype(_BF16)
    o_ref[...] += jnp.dot(w_ref[...], v_ref[...], preferred_element_type=_F32)


def _peer_dense(h2, ri, rj, rg, x1, u16, v16):
    n = h2.shape[0]
    tm, eb = _PEER_TM, _PEER_EB
    slots = _PEER_HEADS * _PEER_TOPK
    nexp = u16.shape[0]
    tok = lambda w: pl.BlockSpec((tm, w), lambda i, e: (i, 0))
    return pl.pallas_call(
        _peer_body,
        grid=(n // tm, nexp // eb),
        in_specs=[tok(_D), tok(slots), tok(slots), tok(slots), tok(_D),
                  pl.BlockSpec((eb, _D), lambda i, e: (e, 0)),
                  pl.BlockSpec((eb, _D), lambda i, e: (e, 0))],
        out_specs=tok(_D),
        out_shape=jax.ShapeDtypeStruct((n, _D), _F32),
        scratch_shapes=[pltpu.VMEM((_PEER_KEYS * _PEER_PITCH, _PEER_KEYS), jnp.uint32),
                        pltpu.VMEM((tm, eb), _BF16)],
        compiler_params=_cparams("parallel", "arbitrary"),
        name="peer_dense",
    )(h2, ri, rj, rg, x1, u16, v16)


def _prepare(norm_mix_g, w_in, conv_w, conv_b, dt_bias, a_log, d_skip, ssd_norm_g, w_a,
             q_norm_g, k_norm_g, w_b, w_o, norm_ffn_g, w_query, sub_keys, expert_u, expert_v):
    xbc_end = _SSD_INNER + _CONV_DIM
    dt_end = xbc_end + 2 * _SSD_HEADS
    pad_dt = lambda a: jnp.pad(a.astype(_F32).reshape(1, 2 * _SSD_HEADS), ((0, 0), (0, _DT_W - 2 * _SSD_HEADS)))
    row = lambda a: a.astype(_F32)[None, :]
    return dict(
        w_perm=jnp.concatenate(
            [w_in[:, :xbc_end], w_in[:, dt_end:], w_in[:, xbc_end:dt_end],
             jnp.zeros((_D, _DT_W - 2 * _SSD_HEADS), w_in.dtype)], axis=1).astype(_BF16),
        norm_mix_g=row(norm_mix_g), conv_w=conv_w.astype(_F32), conv_b=row(conv_b),
        dt_bias=pad_dt(dt_bias), a_log=pad_dt(a_log), d_skip=jnp.repeat(d_skip.astype(_F32), _SSD_HEAD_DIM)[None, :],
        ssd_norm_g=row(ssd_norm_g), q_norm_g=q_norm_g, k_norm_g=k_norm_g,
        w_a=w_a.astype(_BF16), w_b=w_b.astype(_BF16), w_o=w_o.astype(_BF16), norm_ffn_g=row(norm_ffn_g),
        w_query=w_query.astype(_BF16),
        keys16=sub_keys.reshape(2 * _PEER_HEADS, _PEER_KEYS, sub_keys.shape[-1]).astype(_BF16),
        u16=expert_u.astype(_BF16), v16=expert_v.astype(_BF16),
    )


def _layer(x, w):
    bsz, seq, _ = x.shape
    n = bsz * seq
    x2d = x.reshape(n, _D)
    z, xbc, qkv, gates, dt = _in_proj(x2d, w["norm_mix_g"], w["w_perm"], tm=512)
    qkv_groups = _qk_prep(qkv, w["q_norm_g"], w["k_norm_g"], bsz, seq, tm=512)
    act, yf = _ssd_fwd(xbc, dt, w["conv_w"], w["conv_b"], w["dt_bias"], w["a_log"], w["d_skip"], bsz, seq)
    yn = _ssd_bwd(act, dt, w["dt_bias"], w["a_log"], yf, z, w["ssd_norm_g"], bsz, seq)
    attn = []
    for gi, (window, dil) in enumerate(_ATT_GROUPS):
        assert window // (2 * dil) == _ATT_RADIUS
        attn.extend(_attention_group(*qkv_groups[3 * gi:3 * gi + 3], bsz, seq, dil))
    x1, h2, qp = _finish(yn.reshape(n, _SSD_INNER), attn, gates, x2d, w["w_a"], w["w_b"], w["w_o"],
                         w["norm_ffn_g"], w["w_query"], seq, tm=512)
    ri, rj, rg = _peer_route(qp, w["keys16"])
    out = _peer_dense(h2, ri, rj, rg, x1, w["u16"], w["v16"])
    return out.reshape(bsz, seq, _D)


def kernel(x_prompt, x_sample, norm_mix_g, w_in, conv_w, conv_b, dt_bias, a_log, d_skip, ssd_norm_g, w_a,
           q_norm_g, k_norm_g, w_b, w_o, norm_ffn_g, w_query, sub_keys, expert_u, expert_v):
    weights = (norm_mix_g, w_in, conv_w, conv_b, dt_bias, a_log, d_skip, ssd_norm_g, w_a,
               q_norm_g, k_norm_g, w_b, w_o, norm_ffn_g, w_query, sub_keys, expert_u, expert_v)
    outs = [x_prompt, x_sample]
    for l in range(w_in.shape[0]):
        w = _prepare(*(a[l] for a in weights))
        outs = [_layer(x, w) for x in outs]
    return tuple(outs)
```

```python
import functools
import math

import jax
import jax.numpy as jnp
from jax import lax
from jax.experimental import pallas as pl
from jax.experimental.pallas import tpu as pltpu

_F32 = jnp.float32
_BF16 = jnp.bfloat16
_HIGHEST = lax.Precision.HIGHEST

_V7X_VMEM_BYTES = 64 * 1024 * 1024
_VMEM_LIMIT = _V7X_VMEM_BYTES - 8 * 1024 * 1024
_LANES = 128

_EPS = 1e-6
_D = 1024
_SSD_HEADS = 32
_SSD_HEAD_DIM = 64
_SSD_INNER = 2048
_SSD_GROUPS = 4
_SSD_STATE = 128
_SSD_BC = 512
_CONV_W = 5
_CONV_DIM = 3072
_CHUNK = 128
_HALO = 16
_ATT_GROUPS = ((128, 1), (512, 4), (2048, 16))
_ATT_WIDTH = 768
_ATT_OUT = 256
_ATT_HD = 64
_ATT_RADIUS = 64
_ROPE_DIMS = 16
_ROPE_THETA = 500000.0
_NEG_BIG = -1e30
_PEER_HEADS = 8
_PEER_KEYS = 128
_PEER_TOPK = 16
_QKV_W = 3 * _ATT_WIDTH
_GATE_W = 2 * _D
_DT_W = 128
_IN_COLS_PAD = _SSD_INNER + _CONV_DIM + _QKV_W + _GATE_W + _DT_W


def _cparams(*sem):
    return pltpu.CompilerParams(dimension_semantics=sem, vmem_limit_bytes=_VMEM_LIMIT)


def _sigmoid(x):
    return 1.0 / (1.0 + jnp.exp(-x))


_COL_CHUNK = 512


def _inproj_body(x_ref, g_ref, w_ref, z_ref, xbc_ref, qkv_ref, gate_ref, dt_ref):
    x = x_ref[...]
    y = x * lax.rsqrt(jnp.mean(x * x, axis=-1, keepdims=True) + _EPS)
    h = (y * g_ref[...]).astype(_BF16)
    col = 0
    for ref in (z_ref, xbc_ref, qkv_ref, gate_ref, dt_ref):
        width = ref.shape[-1]
        for c0 in range(0, width, _COL_CHUNK):
            cw = min(_COL_CHUNK, width - c0)
            acc = jnp.dot(h, w_ref[:, col + c0:col + c0 + cw], preferred_element_type=_F32)
            ref[:, c0:c0 + cw] = acc.astype(ref.dtype)
        col += width


def _in_proj(x2d, g, w_perm, tm):
    n = x2d.shape[0]
    widths = (_SSD_INNER, _CONV_DIM, _QKV_W, _GATE_W, _DT_W)
    dtypes = (_BF16, _BF16, _BF16, _BF16, _F32)
    return pl.pallas_call(
        _inproj_body,
        grid=(n // tm,),
        in_specs=[
            pl.BlockSpec((tm, _D), lambda i: (i, 0)),
            pl.BlockSpec((1, _D), lambda i: (0, 0)),
            pl.BlockSpec((_D, _IN_COLS_PAD), lambda i: (0, 0), pipeline_mode=pl.Buffered(1)),
        ],
        out_specs=[pl.BlockSpec((tm, w), lambda i: (i, 0)) for w in widths],
        out_shape=[jax.ShapeDtypeStruct((n, w), dt) for w, dt in zip(widths, dtypes)],
        compiler_params=_cparams("parallel"),
        name="in_proj",
    )(x2d, g, w_perm)


def _qkprep_body(q_ref, k_ref, v_ref, qg_ref, kg_ref, c_ref, s1_ref, s2_ref, bd_ref, *refs):
    outs, tmp_ref = refs[:-1], refs[-1]
    tm = q_ref.shape[0]
    cosv = c_ref[...]
    s1 = s1_ref[...]
    s2 = s2_ref[...]
    bd = bd_ref[...]
    cols_per_group = _ATT_OUT // _LANES
    for ti, (src, g_ref, scale) in enumerate(((q_ref, qg_ref, _ATT_HD ** -0.5), (k_ref, kg_ref, 1.0), (v_ref, None, 1.0))):
        for p in range(_ATT_WIDTH // _LANES):
            x = src[:, p * _LANES:(p + 1) * _LANES].astype(_F32)
            if g_ref is not None:
                ss = jnp.dot(_split3(x * x), bd, preferred_element_type=_F32)
                y = x * lax.rsqrt(ss * (1.0 / _ATT_HD) + _EPS) * g_ref[...]
                x = y * cosv + pltpu.roll(y, 8, 1) * s1 + pltpu.roll(y, _LANES - 8, 1) * s2
                if scale != 1.0:
                    x = x * scale
            gi, pl_ = divmod(p, cols_per_group)
            dil = _ATT_GROUPS[gi][1]
            dst = outs[3 * gi + ti]
            lanes = slice(pl_ * _LANES, (pl_ + 1) * _LANES)
            if dil == 1:
                dst[0, 0, :, lanes] = x.astype(dst.dtype)
            else:
                tmp_ref[...] = x
                for r in range(dil):
                    dst[0, r, :, lanes] = tmp_ref[pl.ds(r, tm // dil, stride=dil), :].astype(dst.dtype)


def _rope_tables(seq):
    half = _ROPE_DIMS // 2
    inv = _ROPE_THETA ** (-jnp.arange(half, dtype=_F32) * 2.0 / _ROPE_DIMS)
    ang = jnp.arange(seq, dtype=_F32)[:, None] * inv[None, :]
    cos, sin = jnp.cos(ang), jnp.sin(ang)
    ones = jnp.ones((seq, _ATT_HD - _ROPE_DIMS), _F32)
    zeros = jnp.zeros((seq, _ATT_HD - _ROPE_DIMS), _F32)
    zh = jnp.zeros((seq, half), _F32)
    c = jnp.concatenate([cos, cos, ones], axis=1)
    s1 = jnp.concatenate([zh, sin, zeros], axis=1)
    s2 = jnp.concatenate([-sin, zh, zeros], axis=1)
    rep = _LANES // _ATT_HD
    return tuple(jnp.tile(t, (1, rep)) for t in (c, s1, s2))


def _qk_prep(qkv, q_g, k_g, bsz, seq, tm):
    c, s1, s2 = _rope_tables(seq)
    nseq = seq // tm
    lane = jnp.arange(_LANES)
    bd = jnp.tile((lane[:, None] // _ATT_HD == lane[None, :] // _ATT_HD).astype(_BF16), (3, 1))
    rep = _LANES // _ATT_HD
    qg = jnp.tile(q_g.astype(_F32), rep)[None, :]
    kg = jnp.tile(k_g.astype(_F32), rep)[None, :]
    tab = pl.BlockSpec((tm, _LANES), lambda b, i: (i, 0))
    const = lambda shape: pl.BlockSpec(shape, lambda b, i: (0, 0))
    col = lambda cblk: pl.BlockSpec((tm, _ATT_WIDTH), lambda b, i: (b * nseq + i, cblk))
    out_specs, out_shapes = [], []
    for _, dil in _ATT_GROUPS:
        for _ in range(3):
            out_specs.append(pl.BlockSpec((1, dil, tm // dil, _ATT_OUT), lambda b, i: (b, 0, i, 0)))
            out_shapes.append(jax.ShapeDtypeStruct((bsz, dil, seq // dil, _ATT_OUT), _BF16))
    return pl.pallas_call(
        _qkprep_body,
        grid=(bsz, nseq),
        in_specs=[col(0), col(1), col(2), const((1, _LANES)), const((1, _LANES)), tab, tab, tab,
                  const((3 * _LANES, _LANES))],
        out_specs=out_specs,
        out_shape=out_shapes,
        scratch_shapes=[pltpu.VMEM((tm, _LANES), _F32)],
        compiler_params=_cparams("parallel", "parallel"),
        name="qk_prep",
    )(qkv, qkv, qkv, qg, kg, c, s1, s2, bd)


def _pair_bcast(a, lane_lo, h):
    rows = a.shape[0]
    lo = jnp.broadcast_to(a[:, h:h + 1], (rows, _LANES))
    hi = jnp.broadcast_to(a[:, h + 1:h + 2], (rows, _LANES))
    return jnp.where(lane_lo, lo, hi)


def _split3(x):
    hi = x.astype(_BF16)
    r1 = x - hi.astype(_F32)
    mid = r1.astype(_BF16)
    lo = (r1 - mid.astype(_F32)).astype(_BF16)
    return jnp.concatenate([hi, mid, lo], axis=1)


def _ssd_chunk(act_ref, dt, da, rev, epair_ref, state_ref, xsc_ref, emit):
    L = _CHUNK
    row = lax.broadcasted_iota(jnp.int32, (L, L), 0)
    col = lax.broadcasted_iota(jnp.int32, (L, L), 1)
    keep = (row <= col) if rev else (row >= col)
    keep_t = (row >= col) if rev else (row <= col)
    cs = jnp.dot(keep.astype(_F32), da, precision=_HIGHEST, preferred_element_type=_F32)
    cs_t = jnp.dot(da.T, keep_t.astype(_F32), precision=_HIGHEST, preferred_element_type=_F32)
    dt_t = dt.T
    cs_end = cs[0:1, :] if rev else cs[L - 1:L, :]
    cdec = jnp.exp(cs_end)
    ecs = jnp.exp(cs)
    wst = jnp.exp(cs_end - cs) * dt
    lane_lo = lax.broadcasted_iota(jnp.int32, (1, _LANES), 1) < _SSD_HEAD_DIM
    if epair_ref is not None:
        ecs_all = jnp.dot(_split3(ecs), epair_ref[...], preferred_element_type=_F32)
        wst_all = jnp.dot(_split3(wst), epair_ref[...], preferred_element_type=_F32)
        xsc_ref[...] = (act_ref[:, 0:_SSD_INNER] * wst_all).astype(_BF16)
    heads_per_group = _SSD_HEADS // _SSD_GROUPS
    pairs_per_group = heads_per_group // 2
    gw = heads_per_group * _SSD_HEAD_DIM
    for g in range(_SSD_GROUPS):
        bm = act_ref[:, _SSD_INNER + g * _SSD_STATE:_SSD_INNER + (g + 1) * _SSD_STATE]
        cm = act_ref[:, _SSD_INNER + _SSD_BC + g * _SSD_STATE:_SSD_INNER + _SSD_BC + (g + 1) * _SSD_STATE]
        bm16 = bm.astype(_BF16)
        cm16 = cm.astype(_BF16)
        cb = lax.dot_general(cm16, bm16, (((1,), (1,)), ((), ())), preferred_element_type=_F32)
        y_off = jnp.dot(cm16, state_ref[g].astype(_BF16), preferred_element_type=_F32)
        cdec_cols = []
        for jj in range(pairs_per_group):
            j = g * pairs_per_group + jj
            sl = slice(j * _LANES, (j + 1) * _LANES)
            xs = act_ref[:, sl]
            xs16 = xs.astype(_BF16)
            res = []
            for u in range(2):
                h = 2 * j + u
                diff = jnp.broadcast_to(cs[:, h:h + 1], (L, L)) - cs_t[h:h + 1, :]
                decay = jnp.exp(jnp.where(keep, diff, _NEG_BIG))
                m = (cb * decay * dt_t[h:h + 1, :]).astype(_BF16)
                res.append(jnp.dot(m, xs16, preferred_element_type=_F32))
            y = jnp.where(lane_lo, res[0], res[1])
            if epair_ref is not None:
                ec = ecs_all[:, sl]
            else:
                ec = _pair_bcast(ecs, lane_lo, 2 * j)
                xsc_ref[:, sl] = (xs * _pair_bcast(wst, lane_lo, 2 * j)).astype(_BF16)
            emit(j, y + y_off[:, jj * _LANES:(jj + 1) * _LANES] * ec)
            cdec_cols.append(_pair_bcast(cdec, lane_lo, 2 * j))
        new = jnp.dot(bm.T.astype(_BF16), xsc_ref[:, g * gw:(g + 1) * gw], preferred_element_type=_F32)
        state_ref[g] = state_ref[g] * jnp.concatenate(cdec_cols, axis=1) + new


def _softplus(x):
    return jnp.maximum(x, 0.0) + jnp.log(1.0 + jnp.exp(-jnp.abs(x)))


def _ssd_fwd_body(xp_ref, xc_ref, xn_ref, dt_ref, cw_ref, cb_ref, dtb_ref, alog_ref, dskip_ref,
                  act_out_ref, y_ref, win_ref, act_ref, state_ref, xsc_ref, *, nchunks):
    c = pl.program_id(1)

    @pl.when(c == 0)
    def _():
        state_ref[...] = jnp.zeros_like(state_ref)

    L = _CHUNK
    prev_ok = (c > 0).astype(_F32)
    next_ok = (c < nchunks - 1).astype(_F32)
    win_ref[0:_HALO, :] = xp_ref[0].astype(_F32) * prev_ok
    win_ref[_HALO:_HALO + L, :] = xc_ref[0].astype(_F32)
    win_ref[_HALO + L:2 * _HALO + L, :] = xn_ref[0].astype(_F32) * next_ok
    pad = (_CONV_W - 1) // 2
    cw = cw_ref[...]
    for c0 in range(0, _CONV_DIM, _COL_CHUNK):
        sl = slice(c0, c0 + _COL_CHUNK)
        acc = jnp.broadcast_to(cb_ref[:, sl], (L, _COL_CHUNK))
        for k in range(_CONV_W):
            acc = acc + win_ref[_HALO - pad + k:_HALO - pad + k + L, sl] * cw[k:k + 1, sl]
        a = acc * _sigmoid(acc)
        act_ref[:, sl] = a
        act_out_ref[0, :, sl] = a.astype(act_out_ref.dtype)
    dt = _softplus(dt_ref[0] + dtb_ref[...])
    da = dt * (-jnp.exp(alog_ref[...]))

    def emit(j, y):
        sl = slice(j * _LANES, (j + 1) * _LANES)
        y_ref[0, :, sl] = y + dskip_ref[:, sl] * act_ref[:, sl]

    _ssd_chunk(act_ref, dt, da, False, None, state_ref, xsc_ref, emit)


def _ssd_bwd_body(act_in_ref, dt_ref, dtb_ref, alog_ref, yf_ref, z_ref, ng_ref, epair_ref,
                  yn_ref, act_ref, state_ref, xsc_ref, ytot_ref):
    c = pl.program_id(1)

    @pl.when(c == 0)
    def _():
        state_ref[...] = jnp.zeros_like(state_ref)

    act_ref[...] = act_in_ref[0].astype(_F32)
    shift = _LANES - _SSD_HEADS
    dt = _softplus(pltpu.roll(dt_ref[0], shift, 1) + pltpu.roll(dtb_ref[...], shift, 1))
    da = dt * (-jnp.exp(pltpu.roll(alog_ref[...], shift, 1)))

    def emit(j, y):
        sl = slice(j * _LANES, (j + 1) * _LANES)
        z = z_ref[0, :, sl].astype(_F32)
        ytot_ref[:, sl] = (y + yf_ref[0, :, sl]) * (z * _sigmoid(z))

    _ssd_chunk(act_ref, dt, da, True, epair_ref, state_ref, xsc_ref, emit)
    gw = _SSD_INNER // _SSD_GROUPS
    for g in range(_SSD_GROUPS):
        t = ytot_ref[:, g * gw:(g + 1) * gw]
        r = lax.rsqrt(jnp.mean(t * t, axis=-1, keepdims=True) + _EPS)
        yn_ref[0, :, g * gw:(g + 1) * gw] = (t * r * ng_ref[:, g * gw:(g + 1) * gw]).astype(yn_ref.dtype)


def _spread_matrix():
    r = jnp.arange(3 * _LANES)[:, None] % _LANES
    return (r == jnp.arange(_SSD_INNER)[None, :] // _SSD_HEAD_DIM).astype(_BF16)


def _ssd_scratch():
    return [
        pltpu.VMEM((_CHUNK, _CONV_DIM), _F32),
        pltpu.VMEM((_SSD_GROUPS, _SSD_STATE, _SSD_INNER // _SSD_GROUPS), _F32),
        pltpu.VMEM((_CHUNK, _SSD_INNER), _BF16),
    ]


def _ssd_fwd(xbc, dt, conv_w, conv_b, dtb, alog, dskip, bsz, seq):
    nchunks = seq // _CHUNK
    hb = _CHUNK // _HALO
    xbc3 = xbc.reshape(bsz, seq, _CONV_DIM)
    dt3 = dt.reshape(bsz, seq, _DT_W)
    const = lambda shape: pl.BlockSpec(shape, lambda b, c: (0,) * len(shape))
    act, yf = pl.pallas_call(
        functools.partial(_ssd_fwd_body, nchunks=nchunks),
        grid=(bsz, nchunks),
        in_specs=[
            pl.BlockSpec((1, _HALO, _CONV_DIM), lambda b, c: (b, jnp.maximum(c * hb - 1, 0), 0)),
            pl.BlockSpec((1, _CHUNK, _CONV_DIM), lambda b, c: (b, c, 0)),
            pl.BlockSpec((1, _HALO, _CONV_DIM), lambda b, c: (b, jnp.minimum((c + 1) * hb, nchunks * hb - 1), 0)),
            pl.BlockSpec((1, _CHUNK, _DT_W), lambda b, c: (b, c, 0)),
            const((_CONV_W, _CONV_DIM)), const((1, _CONV_DIM)), const((1, _DT_W)), const((1, _DT_W)),
            const((1, _SSD_INNER)),
        ],
        out_specs=[
            pl.BlockSpec((1, _CHUNK, _CONV_DIM), lambda b, c: (b, c, 0)),
            pl.BlockSpec((1, _CHUNK, _SSD_INNER), lambda b, c: (b, c, 0)),
        ],
        out_shape=[
            jax.ShapeDtypeStruct((bsz, seq, _CONV_DIM), _BF16),
            jax.ShapeDtypeStruct((bsz, seq, _SSD_INNER), _F32),
        ],
        scratch_shapes=[pltpu.VMEM((_CHUNK + 2 * _HALO, _CONV_DIM), _F32)] + _ssd_scratch(),
        compiler_params=_cparams("parallel", "arbitrary"),
        name="ssd_fwd",
    )(xbc3, xbc3, xbc3, dt3, conv_w, conv_b, dtb, alog, dskip)
    return act, yf


def _ssd_bwd(act, dt, dtb, alog, yf, z, norm_g, bsz, seq):
    nchunks = seq // _CHUNK
    dt3 = dt.reshape(bsz, seq, _DT_W)
    z3 = z.reshape(bsz, seq, _SSD_INNER)
    epair = _spread_matrix()
    const = lambda shape: pl.BlockSpec(shape, lambda b, c: (0,) * len(shape))
    blk = lambda w: pl.BlockSpec((1, _CHUNK, w), lambda b, c: (b, nchunks - 1 - c, 0))
    return pl.pallas_call(
        _ssd_bwd_body,
        grid=(bsz, nchunks),
        in_specs=[blk(_CONV_DIM), blk(_DT_W), const((1, _DT_W)), const((1, _DT_W)),
                  blk(_SSD_INNER), blk(_SSD_INNER), const((1, _SSD_INNER)), const(epair.shape)],
        out_specs=blk(_SSD_INNER),
        out_shape=jax.ShapeDtypeStruct((bsz, seq, _SSD_INNER), _BF16),
        scratch_shapes=_ssd_scratch() + [pltpu.VMEM((_CHUNK, _SSD_INNER), _F32)],
        compiler_params=_cparams("parallel", "arbitrary"),
        name="ssd_bwd",
    )(act, dt3, dtb, alog, yf, z3, norm_g, epair)


_QB = 128


_QSUB = 2


def _attn_body(q_ref, kp_ref, kc_ref, kn_ref, vp_ref, vc_ref, vn_ref, o_ref, lse_ref, *, nstep):
    n = pl.program_id(2)
    ii = lax.broadcasted_iota(jnp.int32, (_QB, _QB), 0)
    jj = lax.broadcasted_iota(jnp.int32, (_QB, _QB), 1)
    dlt = jj - ii
    band = (dlt >= _QB - _ATT_RADIUS, jnp.abs(dlt) <= _ATT_RADIUS, dlt <= _ATT_RADIUS - _QB)
    lane_lo = lax.broadcasted_iota(jnp.int32, (1, _LANES), 1) < _ATT_HD
    nt = (((1,), (1,)), ((), ()))
    npair = _ATT_OUT // _LANES
    sl = [slice(p * _LANES, (p + 1) * _LANES) for p in range(npair)]

    def blocks(pr, cr, nr, p):
        return ([pr[0, 0, :, sl[p]]] + [cr[0, 0, s * _QB:(s + 1) * _QB, sl[p]] for s in range(_QSUB)]
                + [nr[0, 0, :, sl[p]]])

    ks = [blocks(kp_ref, kc_ref, kn_ref, p) for p in range(npair)]
    vs = [blocks(vp_ref, vc_ref, vn_ref, p) for p in range(npair)]
    chains = [(s, p, u) for s in range(_QSUB) for p in range(npair) for u in range(2)]
    scores = []
    for s, p, u in chains:
        qp = q_ref[0, 0, s * _QB:(s + 1) * _QB, sl[p]]
        hm = lane_lo if u == 0 else jnp.logical_not(lane_lo)
        qm = jnp.where(hm, qp, jnp.zeros_like(qp))
        masks = [band[0], band[1], band[2]]
        if s == 0:
            masks[0] = jnp.logical_and(band[0], n > 0)
        if s == _QSUB - 1:
            masks[2] = jnp.logical_and(band[2], n < nstep - 1)
        scores.append([jnp.where(m, lax.dot_general(qm, kx, nt, preferred_element_type=_F32), _NEG_BIG)
                       for m, kx in zip(masks, ks[p][s:s + 3])])
    mxs = [jnp.max(jnp.maximum(jnp.maximum(sc[0], sc[1]), sc[2]), axis=-1, keepdims=True) for sc in scores]
    probs = [[jnp.exp(sx - mx) for sx in sc] for sc, mx in zip(scores, mxs)]
    dens = [jnp.sum(pr[0] + pr[1] + pr[2], axis=-1, keepdims=True) for pr in probs]
    outs = []
    for (s, p, u), pr in zip(chains, probs):
        outs.append(sum(jnp.dot(px.astype(_BF16), vx, preferred_element_type=_F32)
                        for px, vx in zip(pr, vs[p][s:s + 3])))
    for c in range(0, len(chains), 2):
        s, p, _ = chains[c]
        rows = slice(s * _QB, (s + 1) * _QB)
        o0, o1 = outs[c] / dens[c], outs[c + 1] / dens[c + 1]
        l0 = jnp.broadcast_to(mxs[c] + jnp.log(dens[c]), (_QB, _LANES))
        l1 = jnp.broadcast_to(mxs[c + 1] + jnp.log(dens[c + 1]), (_QB, _LANES))
        o_ref[0, 0, rows, sl[p]] = jnp.where(lane_lo, o0, o1)
        lse_ref[0, 0, rows, sl[p]] = jnp.where(lane_lo, l0, l1)


def _attention_group(q, k, v, bsz, seq, dil):
    t = seq // dil
    nstep = t // (_QSUB * _QB)
    nblk = t // _QB
    own = pl.BlockSpec((1, 1, _QSUB * _QB, _ATT_OUT), lambda b, r, n: (b, r, n, 0))
    before = pl.BlockSpec((1, 1, _QB, _ATT_OUT), lambda b, r, n: (b, r, jnp.maximum(_QSUB * n - 1, 0), 0))
    after = pl.BlockSpec((1, 1, _QB, _ATT_OUT), lambda b, r, n: (b, r, jnp.minimum(_QSUB * (n + 1), nblk - 1), 0))
    return pl.pallas_call(
        functools.partial(_attn_body, nstep=nstep),
        grid=(bsz, dil, nstep),
        in_specs=[own, before, own, after, before, own, after],
        out_specs=[own, own],
        out_shape=[jax.ShapeDtypeStruct((bsz, dil, t, _ATT_OUT), _F32)] * 2,
        compiler_params=_cparams("parallel", "parallel", "parallel"),
        name=f"attn_d{dil}",
    )(q, k, k, k, v, v, v)


def _finish_body(yn_ref, o0_ref, l0_ref, o1_ref, l1_ref, o2_ref, l2_ref, gate_ref, x_ref,
                 wa_ref, wb_ref, wo_ref, ng_ref, wq_ref, x1_ref, h2_ref, qp_ref, mix_ref):
    tm = x_ref.shape[0]
    npair = _ATT_OUT // _LANES

    def token_order(ref, slot, dil):
        if dil == 1:
            return [ref[0, 0, :, p * _LANES:(p + 1) * _LANES] for p in range(npair)]
        for p in range(npair):
            for r in range(dil):
                mix_ref[slot * npair + p, pl.ds(r, tm // dil, stride=dil), :] = ref[0, r, :, p * _LANES:(p + 1) * _LANES]
        return [mix_ref[slot * npair + p] for p in range(npair)]

    dils = [d for _, d in _ATT_GROUPS]
    o_parts = [token_order(r, 2 * g, dils[g]) for g, r in enumerate((o0_ref, o1_ref, o2_ref))]
    l_parts = [token_order(r, 2 * g + 1, dils[g]) for g, r in enumerate((l0_ref, l1_ref, l2_ref))]
    halves = []
    for p in range(npair):
        l0, l1, l2 = l_parts[0][p], l_parts[1][p], l_parts[2][p]
        lm = jnp.maximum(jnp.maximum(l0, l1), l2)
        e0, e1, e2 = jnp.exp(l0 - lm), jnp.exp(l1 - lm), jnp.exp(l2 - lm)
        den = e0 + e1 + e2
        halves.append((e0 / den) * o_parts[0][p] + (e1 / den) * o_parts[1][p] + (e2 / den) * o_parts[2][p])
    o = jnp.concatenate(halves, axis=1)
    a = jnp.dot(yn_ref[...], wa_ref[...], preferred_element_type=_F32)
    b = jnp.dot(o.astype(_BF16), wb_ref[...], preferred_element_type=_F32)
    ga = _sigmoid(gate_ref[:, 0:_D].astype(_F32))
    gb = _sigmoid(gate_ref[:, _D:2 * _D].astype(_F32))
    merged = (ga * a + gb * b).astype(_BF16)
    x1 = x_ref[...] + jnp.dot(merged, wo_ref[...], preferred_element_type=_F32)
    x1_ref[...] = x1
    y = x1 * lax.rsqrt(jnp.mean(x1 * x1, axis=-1, keepdims=True) + _EPS)
    h2 = (y * ng_ref[...]).astype(_BF16)
    h2_ref[...] = h2
    for head in range(_PEER_HEADS):
        sl = slice(2 * head * _PEER_KEYS, 2 * (head + 1) * _PEER_KEYS)
        q2 = jnp.dot(h2, wq_ref[:, sl], preferred_element_type=_F32).astype(qp_ref.dtype)
        qp_ref[2 * head] = q2[:, 0:_PEER_KEYS]
        qp_ref[2 * head + 1] = q2[:, _PEER_KEYS:2 * _PEER_KEYS]


def _finish(yn, attn, gates, x2d, wa, wb, wo, ng, wq, seq, tm):
    n = x2d.shape[0]
    nseq = seq // tm
    row = lambda w: pl.BlockSpec((tm, w), lambda i: (i, 0))
    const = lambda shape: pl.BlockSpec(shape, lambda i: (0, 0), pipeline_mode=pl.Buffered(1))
    cls = lambda dil: pl.BlockSpec((1, dil, tm // dil, _ATT_OUT), lambda i: (i // nseq, 0, i % nseq, 0))
    attn_specs = [cls(dil) for _, dil in _ATT_GROUPS for _ in range(2)]
    nq = 2 * _PEER_HEADS
    return pl.pallas_call(
        _finish_body,
        grid=(n // tm,),
        in_specs=[row(_SSD_INNER)] + attn_specs + [row(_GATE_W), row(_D),
                  const((_SSD_INNER, _D)), const((_ATT_OUT, _D)), const((_D, _D)), const((1, _D)),
                  const((_D, nq * _PEER_KEYS))],
        out_specs=[row(_D), row(_D), pl.BlockSpec((nq, tm, _PEER_KEYS), lambda i: (0, i, 0))],
        out_shape=[jax.ShapeDtypeStruct((n, _D), _F32), jax.ShapeDtypeStruct((n, _D), _BF16),
                   jax.ShapeDtypeStruct((nq, n, _PEER_KEYS), _BF16)],
        scratch_shapes=[pltpu.VMEM((2 * len(_ATT_GROUPS) * (_ATT_OUT // _LANES), tm, _LANES), _F32)],
        compiler_params=_cparams("parallel"),
        name="finish",
    )(yn, *attn, gates, x2d, wa, wb, wo, ng, wq)


_ROUTE_TB = 1024
_ROUTE_SUB = _ROUTE_TB // _LANES
_ROUTE_CHAINS = 4


def _sweep_top(ref, nrows, k, codes_ref=None):
    sub = _ROUTE_SUB
    neg = jnp.full((sub, _LANES), -jnp.inf, _F32)
    vals, outs = [], []
    prev = None
    for r in range(k):
        best = [None] * _ROUTE_CHAINS
        for row in range(nrows):
            v = ref[row * sub:(row + 1) * sub, :]
            if prev is not None:
                v = jnp.where(prev == row, neg, v)
                ref[row * sub:(row + 1) * sub, :] = v
            c = row % _ROUTE_CHAINS
            rid = jnp.full((sub, _LANES), row, jnp.int32)
            code = codes_ref[row * sub:(row + 1) * sub, :] if codes_ref is not None else None
            if best[c] is None:
                best[c] = (v, rid, code)
            else:
                bv, bk, bc = best[c]
                better = v > bv
                best[c] = (jnp.maximum(bv, v), jnp.where(better, rid, bk),
                           jnp.where(better, code, bc) if code is not None else None)
        acc = best[0]
        for c in range(1, min(_ROUTE_CHAINS, nrows)):
            av, ak, ac = acc
            bv, bk, bc = best[c]
            take = jnp.logical_or(bv > av, jnp.logical_and(bv == av, bk < ak))
            acc = (jnp.maximum(av, bv), jnp.where(take, bk, ak), jnp.where(take, bc, ac) if ac is not None else None)
        vals.append(acc[0])
        outs.append(acc[2] if codes_ref is not None else acc[1])
        prev = acc[1]
    return vals, outs


def _route_body(qp_ref, keys_ref, i_ref, j_ref, g_ref, s_ref, cand_ref, code_ref, is_ref, js_ref, gs_ref):
    nt = (((1,), (1,)), ((), ()))
    k = _PEER_TOPK
    sub = _ROUTE_SUB
    pairs = [(a, b) for a in range(k) for b in range(k // (a + 1))]

    def head(h, carry):
        hv, hi = [], []
        for c in range(2):
            for a in range(sub):
                tile = lax.dot_general(keys_ref[2 * h + c], qp_ref[2 * h + c, a * _LANES:(a + 1) * _LANES, :], nt,
                                       preferred_element_type=_F32)
                s_ref[c, pl.ds(a, _PEER_KEYS, stride=sub), :] = tile
            v, i = _sweep_top(s_ref.at[c], _PEER_KEYS, k)
            hv.append(v)
            hi.append(i)
        for p, (a, b) in enumerate(pairs):
            cand_ref[p * sub:(p + 1) * sub, :] = hv[0][a] + hv[1][b]
            code_ref[p * sub:(p + 1) * sub, :] = hi[0][a] * _PEER_KEYS + hi[1][b]
        fs, codes = _sweep_top(cand_ref, len(pairs), k, codes_ref=code_ref)
        e = [jnp.exp(f - fs[0]) for f in fs]
        den = e[0]
        for x in e[1:]:
            den = den + x
        for r in range(k):
            r0 = pl.multiple_of((h * k + r) * sub, sub)
            is_ref[pl.ds(r0, sub), :] = jnp.right_shift(codes[r], 7).astype(_F32)
            js_ref[pl.ds(r0, sub), :] = jnp.bitwise_and(codes[r], _PEER_KEYS - 1).astype(_F32)
            gs_ref[pl.ds(r0, sub), :] = e[r] / den
        return carry

    lax.fori_loop(0, _PEER_HEADS, head, 0)
    slots = _PEER_HEADS * k
    for src, dst in ((is_ref, i_ref), (js_ref, j_ref), (gs_ref, g_ref)):
        for a in range(sub):
            dst[a * _LANES:(a + 1) * _LANES, :] = src[pl.ds(a, slots, stride=sub), :].T


def _peer_route(qp, sub_keys16):
    nq, n, _ = qp.shape
    tq = _ROUTE_TB
    slots = _PEER_HEADS * _PEER_TOPK
    ncand = sum(_PEER_TOPK // (a + 1) for a in range(_PEER_TOPK))
    out = pl.BlockSpec((tq, slots), lambda i: (i, 0))
    return pl.pallas_call(
        _route_body,
        grid=(n // tq,),
        in_specs=[pl.BlockSpec((nq, tq, _PEER_KEYS), lambda i: (0, i, 0)),
                  pl.BlockSpec((nq, _PEER_KEYS, _PEER_KEYS), lambda i: (0, 0, 0))],
        out_specs=[out] * 3,
        out_shape=[jax.ShapeDtypeStruct((n, slots), _F32)] * 3,
        scratch_shapes=[pltpu.VMEM((2, _PEER_KEYS * _ROUTE_SUB, _LANES), _F32),
                        pltpu.VMEM((ncand * _ROUTE_SUB, _LANES), _F32),
                        pltpu.VMEM((ncand * _ROUTE_SUB, _LANES), jnp.int32)]
        + [pltpu.VMEM((slots * _ROUTE_SUB, _LANES), _F32)] * 3,
        compiler_params=_cparams("parallel"),
        name="peer_route",
    )(qp, sub_keys16)


_PEER_TM = 512
_PEER_HALF = _PEER_TM // 2
_PEER_EB = 2048
_PEER_CW = 256
_PEER_PITCH = _PEER_HALF + 8
_PEER_UNROLL = 32


def _peer_body(h_ref, i_ref, j_ref, g_ref, x1_ref, u_ref, v_ref, o_ref, gate_ref, w_ref):
    e = pl.program_id(1)
    slots = _PEER_HEADS * _PEER_TOPK
    nt = (((1,), (1,)), ((), ()))
    hi_mask = jnp.uint32(0xFFFF0000)

    @pl.when(e == 0)
    def _():
        o_ref[...] = x1_ref[...]
        sub = lax.broadcasted_iota(jnp.int32, (_PEER_KEYS, slots), 0).astype(_F32).astype(_BF16)
        one = jnp.ones((), _BF16)
        zero = jnp.zeros((), _BF16)

        def gate_bits(n):
            irow = i_ref[pl.ds(n, 1), :].astype(_BF16)
            jrow = j_ref[pl.ds(n, 1), :].astype(_BF16)
            grow = g_ref[pl.ds(n, 1), :].astype(_BF16)
            p_t = jnp.where(sub == irow, one, zero)
            q_t = jnp.where(sub == jrow, grow, zero)
            gm = lax.dot_general(p_t, q_t, nt, preferred_element_type=_F32)
            return pltpu.bitcast(gm, jnp.uint32)

        def pair(n, carry):
            lo = jnp.right_shift(gate_bits(n), jnp.uint32(16))
            hi = jnp.bitwise_and(gate_bits(n + _PEER_HALF), hi_mask)
            gate_ref[pl.ds(n, _PEER_KEYS, stride=_PEER_PITCH), :] = jnp.bitwise_or(lo, hi)
            return carry

        lax.fori_loop(0, _PEER_HALF, pair, 0, unroll=_PEER_UNROLL)

    rows_per_chunk = _PEER_CW // _PEER_KEYS
    h = h_ref[...]
    for c in range(_PEER_EB // _PEER_CW):
        hmat = lax.dot_general(h, u_ref[c * _PEER_CW:(c + 1) * _PEER_CW, :], nt, preferred_element_type=_F32)
        act = 0.5 * hmat * (1.0 + lax.erf(hmat * (1.0 / math.sqrt(2.0))))
        cols = []
        for r in range(rows_per_chunk):
            row = e * (_PEER_EB // _PEER_KEYS) + c * rows_per_chunk + r
            word = gate_ref[pl.ds(pl.multiple_of(row * _PEER_PITCH, 8), _PEER_HALF), :]
            lo = pltpu.bitcast(jnp.left_shift(word, jnp.uint32(16)), _F32)
            hi = pltpu.bitcast(jnp.bitwise_and(word, hi_mask), _F32)
            cols.append(jnp.concatenate([lo, hi], axis=0))
        gate = jnp.concatenate(cols, axis=1)
        w_ref[:, c * _PEER_CW:(c + 1) * _PEER_CW] = (gate * act).astype(_BF16)
    o_ref[...] += jnp.dot(w_ref[...], v_ref[...], preferred_element_type=_F32)


def _peer_dense(h2, ri, rj, rg, x1, u16, v16):
    n = h2.shape[0]
    tm, eb = _PEER_TM, _PEER_EB
    slots = _PEER_HEADS * _PEER_TOPK
    nexp = u16.shape[0]
    tok = lambda w: pl.BlockSpec((tm, w), lambda i, e: (i, 0))
    return pl.pallas_call(
        _peer_body,
        grid=(n // tm, nexp // eb),
        in_specs=[tok(_D), tok(slots), tok(slots), tok(slots), tok(_D),
                  pl.BlockSpec((eb, _D), lambda i, e: (e, 0)),
                  pl.BlockSpec((eb, _D), lambda i, e: (e, 0))],
        out_specs=tok(_D),
        out_shape=jax.ShapeDtypeStruct((n, _D), _F32),
        scratch_shapes=[pltpu.VMEM((_PEER_KEYS * _PEER_PITCH, _PEER_KEYS), jnp.uint32),
                        pltpu.VMEM((tm, eb), _BF16)],
        compiler_params=_cparams("parallel", "arbitrary"),
        name="peer_dense",
    )(h2, ri, rj, rg, x1, u16, v16)


def _prepare(norm_mix_g, w_in, conv_w, conv_b, dt_bias, a_log, d_skip, ssd_norm_g, w_a,
             q_norm_g, k_norm_g, w_b, w_o, norm_ffn_g, w_query, sub_keys, expert_u, expert_v):
    xbc_end = _SSD_INNER + _CONV_DIM
    dt_end = xbc_end + 2 * _SSD_HEADS
    pad_dt = lambda a: jnp.pad(a.astype(_F32).reshape(1, 2 * _SSD_HEADS), ((0, 0), (0, _DT_W - 2 * _SSD_HEADS)))
    row = lambda a: a.astype(_F32)[None, :]
    return dict(
        w_perm=jnp.concatenate(
            [w_in[:, :xbc_end], w_in[:, dt_end:], w_in[:, xbc_end:dt_end],
             jnp.zeros((_D, _DT_W - 2 * _SSD_HEADS), w_in.dtype)], axis=1).astype(_BF16),
        norm_mix_g=row(norm_mix_g), conv_w=conv_w.astype(_F32), conv_b=row(conv_b),
        dt_bias=pad_dt(dt_bias), a_log=pad_dt(a_log), d_skip=jnp.repeat(d_skip.astype(_F32), _SSD_HEAD_DIM)[None, :],
        ssd_norm_g=row(ssd_norm_g), q_norm_g=q_norm_g, k_norm_g=k_norm_g,
        w_a=w_a.astype(_BF16), w_b=w_b.astype(_BF16), w_o=w_o.astype(_BF16), norm_ffn_g=row(norm_ffn_g),
        w_query=w_query.astype(_BF16),
        keys16=sub_keys.reshape(2 * _PEER_HEADS, _PEER_KEYS, sub_keys.shape[-1]).astype(_BF16),
        u16=expert_u.astype(_BF16), v16=expert_v.astype(_BF16),
    )


def _layer(x, w):
    bsz, seq, _ = x.shape
    n = bsz * seq
    x2d = x.reshape(n, _D)
    z, xbc, qkv, gates, dt = _in_proj(x2d, w["norm_mix_g"], w["w_perm"], tm=512)
    qkv_groups = _qk_prep(qkv, w["q_norm_g"], w["k_norm_g"], bsz, seq, tm=512)
    act, yf = _ssd_fwd(xbc, dt, w["conv_w"], w["conv_b"], w["dt_bias"], w["a_log"], w["d_skip"], bsz, seq)
    yn = _ssd_bwd(act, dt, w["dt_bias"], w["a_log"], yf, z, w["ssd_norm_g"], bsz, seq)
    attn = []
    for gi, (window, dil) in enumerate(_ATT_GROUPS):
        assert window // (2 * dil) == _ATT_RADIUS
        attn.extend(_attention_group(*qkv_groups[3 * gi:3 * gi + 3], bsz, seq, dil))
    x1, h2, qp = _finish(yn.reshape(n, _SSD_INNER), attn, gates, x2d, w["w_a"], w["w_b"], w["w_o"],
                         w["norm_ffn_g"], w["w_query"], seq, tm=512)
    ri, rj, rg = _peer_route(qp, w["keys16"])
    out = _peer_dense(h2, ri, rj, rg, x1, w["u16"], w["v16"])
    return out.reshape(bsz, seq, _D)


def kernel(x_prompt, x_sample, norm_mix_g, w_in, conv_w, conv_b, dt_bias, a_log, d_skip, ssd_norm_g, w_a,
           q_norm_g, k_norm_g, w_b, w_o, norm_ffn_g, w_query, sub_keys, expert_u, expert_v):
    weights = (norm_mix_g, w_in, conv_w, conv_b, dt_bias, a_log, d_skip, ssd_norm_g, w_a,
               q_norm_g, k_norm_g, w_b, w_o, norm_ffn_g, w_query, sub_keys, expert_u, expert_v)
    outs = [x_prompt, x_sample]
    for l in range(w_in.shape[0]):
        w = _prepare(*(a[l] for a in weights))
        outs = [_layer(x, w) for x in outs]
    return tuple(outs)
```

```python
import functools
import math

import jax
import jax.numpy as jnp
from jax import lax
from jax.experimental import pallas as pl
from jax.experimental.pallas import tpu as pltpu

_F32 = jnp.float32
_BF16 = jnp.bfloat16
_HIGHEST = lax.Precision.HIGHEST

_V7X_VMEM_BYTES = 64 * 1024 * 1024
_VMEM_LIMIT = _V7X_VMEM_BYTES - 8 * 1024 * 1024
_LANES = 128
_TOKEN_TILE = 512

_EPS = 1e-6
_D = 1024
_SSD_HEADS = 32
_SSD_HEAD_DIM = 64
_SSD_INNER = 2048
_SSD_GROUPS = 4
_SSD_STATE = 128
_SSD_BC = 512
_CONV_W = 5
_CONV_DIM = 3072
_CHUNK = 128
_HALO = 16
_ATT_GROUPS = ((128, 1), (512, 4), (2048, 16))
_ATT_WIDTH = 768
_ATT_OUT = 256
_ATT_HD = 64
_ATT_RADIUS = 64
_ROPE_DIMS = 16
_ROPE_THETA = 500000.0
_NEG_BIG = -1e30
_PEER_HEADS = 8
_PEER_KEYS = 128
_PEER_TOPK = 16
_QKV_W = 3 * _ATT_WIDTH
_GATE_W = 2 * _D
_DT_W = 128
_IN_COLS_PAD = _SSD_INNER + _CONV_DIM + _QKV_W + _GATE_W + _DT_W


def _cparams(*sem):
    return pltpu.CompilerParams(dimension_semantics=sem, vmem_limit_bytes=_VMEM_LIMIT)


def _sigmoid(x):
    return 1.0 / (1.0 + jnp.exp(-x))


_COL_CHUNK = 512


def _inproj_body(x_ref, g_ref, w_ref, z_ref, xbc_ref, qkv_ref, gate_ref, dt_ref):
    x = x_ref[...]
    y = x * lax.rsqrt(jnp.mean(x * x, axis=-1, keepdims=True) + _EPS)
    h = (y * g_ref[...]).astype(_BF16)
    col = 0
    for ref in (z_ref, xbc_ref, qkv_ref, gate_ref, dt_ref):
        width = ref.shape[-1]
        for c0 in range(0, width, _COL_CHUNK):
            cw = min(_COL_CHUNK, width - c0)
            acc = jnp.dot(h, w_ref[:, col + c0:col + c0 + cw], preferred_element_type=_F32)
            ref[:, c0:c0 + cw] = acc.astype(ref.dtype)
        col += width


def _in_proj(x2d, g, w_perm, tm):
    n = x2d.shape[0]
    widths = (_SSD_INNER, _CONV_DIM, _QKV_W, _GATE_W, _DT_W)
    dtypes = (_BF16, _BF16, _BF16, _BF16, _F32)
    return pl.pallas_call(
        _inproj_body,
        grid=(n // tm,),
        in_specs=[
            pl.BlockSpec((tm, _D), lambda i: (i, 0)),
            pl.BlockSpec((1, _D), lambda i: (0, 0)),
            pl.BlockSpec((_D, _IN_COLS_PAD), lambda i: (0, 0), pipeline_mode=pl.Buffered(1)),
        ],
        out_specs=[pl.BlockSpec((tm, w), lambda i: (i, 0)) for w in widths],
        out_shape=[jax.ShapeDtypeStruct((n, w), dt) for w, dt in zip(widths, dtypes)],
        compiler_params=_cparams("parallel"),
        name="in_proj",
    )(x2d, g, w_perm)


def _qkprep_body(q_ref, k_ref, v_ref, qg_ref, kg_ref, c_ref, s1_ref, s2_ref, bd_ref, *refs):
    outs, tmp_ref = refs[:-1], refs[-1]
    tm = q_ref.shape[0]
    cosv = c_ref[...]
    s1 = s1_ref[...]
    s2 = s2_ref[...]
    bd = bd_ref[...]
    cols_per_group = _ATT_OUT // _LANES
    for ti, (src, g_ref, scale) in enumerate(((q_ref, qg_ref, _ATT_HD ** -0.5), (k_ref, kg_ref, 1.0), (v_ref, None, 1.0))):
        for p in range(_ATT_WIDTH // _LANES):
            x = src[:, p * _LANES:(p + 1) * _LANES].astype(_F32)
            if g_ref is not None:
                ss = jnp.dot(_split3(x * x), bd, preferred_element_type=_F32)
                y = x * lax.rsqrt(ss * (1.0 / _ATT_HD) + _EPS) * g_ref[...]
                x = y * cosv + pltpu.roll(y, 8, 1) * s1 + pltpu.roll(y, _LANES - 8, 1) * s2
                if scale != 1.0:
                    x = x * scale
            gi, pl_ = divmod(p, cols_per_group)
            dil = _ATT_GROUPS[gi][1]
            dst = outs[3 * gi + ti]
            lanes = slice(pl_ * _LANES, (pl_ + 1) * _LANES)
            if dil == 1:
                dst[0, 0, :, lanes] = x.astype(dst.dtype)
            else:
                tmp_ref[...] = x
                for r in range(dil):
                    dst[0, r, :, lanes] = tmp_ref[pl.ds(r, tm // dil, stride=dil), :].astype(dst.dtype)


def _rope_tables(seq):
    half = _ROPE_DIMS // 2
    inv = _ROPE_THETA ** (-jnp.arange(half, dtype=_F32) * 2.0 / _ROPE_DIMS)
    ang = jnp.arange(seq, dtype=_F32)[:, None] * inv[None, :]
    cos, sin = jnp.cos(ang), jnp.sin(ang)
    ones = jnp.ones((seq, _ATT_HD - _ROPE_DIMS), _F32)
    zeros = jnp.zeros((seq, _ATT_HD - _ROPE_DIMS), _F32)
    zh = jnp.zeros((seq, half), _F32)
    c = jnp.concatenate([cos, cos, ones], axis=1)
    s1 = jnp.concatenate([zh, sin, zeros], axis=1)
    s2 = jnp.concatenate([-sin, zh, zeros], axis=1)
    rep = _LANES // _ATT_HD
    return tuple(jnp.tile(t, (1, rep)) for t in (c, s1, s2))


def _qk_prep(qkv, q_g, k_g, bsz, seq, tm):
    c, s1, s2 = _rope_tables(seq)
    nseq = seq // tm
    lane = jnp.arange(_LANES)
    bd = jnp.tile((lane[:, None] // _ATT_HD == lane[None, :] // _ATT_HD).astype(_BF16), (3, 1))
    rep = _LANES // _ATT_HD
    qg = jnp.tile(q_g.astype(_F32), rep)[None, :]
    kg = jnp.tile(k_g.astype(_F32), rep)[None, :]
    tab = pl.BlockSpec((tm, _LANES), lambda b, i: (i, 0))
    const = lambda shape: pl.BlockSpec(shape, lambda b, i: (0, 0))
    col = lambda cblk: pl.BlockSpec((tm, _ATT_WIDTH), lambda b, i: (b * nseq + i, cblk))
    out_specs, out_shapes = [], []
    for _, dil in _ATT_GROUPS:
        for _ in range(3):
            out_specs.append(pl.BlockSpec((1, dil, tm // dil, _ATT_OUT), lambda b, i: (b, 0, i, 0)))
            out_shapes.append(jax.ShapeDtypeStruct((bsz, dil, seq // dil, _ATT_OUT), _BF16))
    return pl.pallas_call(
        _qkprep_body,
        grid=(bsz, nseq),
        in_specs=[col(0), col(1), col(2), const((1, _LANES)), const((1, _LANES)), tab, tab, tab,
                  const((3 * _LANES, _LANES))],
        out_specs=out_specs,
        out_shape=out_shapes,
        scratch_shapes=[pltpu.VMEM((tm, _LANES), _F32)],
        compiler_params=_cparams("parallel", "parallel"),
        name="qk_prep",
    )(qkv, qkv, qkv, qg, kg, c, s1, s2, bd)


def _pair_bcast(a, lane_lo, h):
    rows = a.shape[0]
    lo = jnp.broadcast_to(a[:, h:h + 1], (rows, _LANES))
    hi = jnp.broadcast_to(a[:, h + 1:h + 2], (rows, _LANES))
    return jnp.where(lane_lo, lo, hi)


def _split3(x):
    hi = x.astype(_BF16)
    r1 = x - hi.astype(_F32)
    mid = r1.astype(_BF16)
    lo = (r1 - mid.astype(_F32)).astype(_BF16)
    return jnp.concatenate([hi, mid, lo], axis=1)


def _ssd_chunk(act_ref, dt, da, rev, epair_ref, state_ref, xsc_ref, emit):
    L = _CHUNK
    row = lax.broadcasted_iota(jnp.int32, (L, L), 0)
    col = lax.broadcasted_iota(jnp.int32, (L, L), 1)
    keep = (row <= col) if rev else (row >= col)
    keep_t = (row >= col) if rev else (row <= col)
    cs = jnp.dot(keep.astype(_F32), da, precision=_HIGHEST, preferred_element_type=_F32)
    cs_t = jnp.dot(da.T, keep_t.astype(_F32), precision=_HIGHEST, preferred_element_type=_F32)
    dt_t = dt.T
    cs_end = cs[0:1, :] if rev else cs[L - 1:L, :]
    cdec = jnp.exp(cs_end)
    ecs = jnp.exp(cs)
    wst = jnp.exp(cs_end - cs) * dt
    lane_lo = lax.broadcasted_iota(jnp.int32, (1, _LANES), 1) < _SSD_HEAD_DIM
    if epair_ref is not None:
        ecs_all = jnp.dot(_split3(ecs), epair_ref[...], preferred_element_type=_F32)
        wst_all = jnp.dot(_split3(wst), epair_ref[...], preferred_element_type=_F32)
        xsc_ref[...] = (act_ref[:, 0:_SSD_INNER] * wst_all).astype(_BF16)
    heads_per_group = _SSD_HEADS // _SSD_GROUPS
    pairs_per_group = heads_per_group // 2
    gw = heads_per_group * _SSD_HEAD_DIM
    for g in range(_SSD_GROUPS):
        bm = act_ref[:, _SSD_INNER + g * _SSD_STATE:_SSD_INNER + (g + 1) * _SSD_STATE]
        cm = act_ref[:, _SSD_INNER + _SSD_BC + g * _SSD_STATE:_SSD_INNER + _SSD_BC + (g + 1) * _SSD_STATE]
        bm16 = bm.astype(_BF16)
        cm16 = cm.astype(_BF16)
        cb = lax.dot_general(cm16, bm16, (((1,), (1,)), ((), ())), preferred_element_type=_F32)
        y_off = jnp.dot(cm16, state_ref[g].astype(_BF16), preferred_element_type=_F32)
        cdec_cols = []
        for jj in range(pairs_per_group):
            j = g * pairs_per_group + jj
            sl = slice(j * _LANES, (j + 1) * _LANES)
            xs = act_ref[:, sl]
            xs16 = xs.astype(_BF16)
            res = []
            for u in range(2):
                h = 2 * j + u
                diff = jnp.broadcast_to(cs[:, h:h + 1], (L, L)) - cs_t[h:h + 1, :]
                decay = jnp.exp(jnp.where(keep, diff, _NEG_BIG))
                m = (cb * decay * dt_t[h:h + 1, :]).astype(_BF16)
                res.append(jnp.dot(m, xs16, preferred_element_type=_F32))
            y = jnp.where(lane_lo, res[0], res[1])
            if epair_ref is not None:
                ec = ecs_all[:, sl]
            else:
                ec = _pair_bcast(ecs, lane_lo, 2 * j)
                xsc_ref[:, sl] = (xs * _pair_bcast(wst, lane_lo, 2 * j)).astype(_BF16)
            emit(j, y + y_off[:, jj * _LANES:(jj + 1) * _LANES] * ec)
            cdec_cols.append(_pair_bcast(cdec, lane_lo, 2 * j))
        new = jnp.dot(bm.T.astype(_BF16), xsc_ref[:, g * gw:(g + 1) * gw], preferred_element_type=_F32)
        state_ref[g] = state_ref[g] * jnp.concatenate(cdec_cols, axis=1) + new


def _softplus(x):
    return jnp.maximum(x, 0.0) + jnp.log(1.0 + jnp.exp(-jnp.abs(x)))


_SSD_STEP_CHUNKS = 2
_SSD_STEP = _SSD_STEP_CHUNKS * _CHUNK


def _ssd_fwd_body(xp_ref, xc_ref, xn_ref, dt_ref, cw_ref, cb_ref, dtb_ref, alog_ref, dskip_ref,
                  act_out_ref, y_ref, win_ref, act_ref, state_ref, xsc_ref, *, nsteps):
    c = pl.program_id(1)

    @pl.when(c == 0)
    def _():
        state_ref[...] = jnp.zeros_like(state_ref)

    L = _CHUNK
    prev_ok = (c > 0).astype(_F32)
    next_ok = (c < nsteps - 1).astype(_F32)
    win_ref[0:_HALO, :] = xp_ref[0].astype(_F32) * prev_ok
    win_ref[_HALO:_HALO + _SSD_STEP, :] = xc_ref[0].astype(_F32)
    win_ref[_HALO + _SSD_STEP:2 * _HALO + _SSD_STEP, :] = xn_ref[0].astype(_F32) * next_ok
    pad = (_CONV_W - 1) // 2
    cw = cw_ref[...]
    a_neg = -jnp.exp(alog_ref[...])
    for s in range(_SSD_STEP_CHUNKS):
        rows = slice(s * L, (s + 1) * L)
        for c0 in range(0, _CONV_DIM, _COL_CHUNK):
            sl = slice(c0, c0 + _COL_CHUNK)
            acc = jnp.broadcast_to(cb_ref[:, sl], (L, _COL_CHUNK))
            for k in range(_CONV_W):
                r0 = _HALO + s * L - pad + k
                acc = acc + win_ref[r0:r0 + L, sl] * cw[k:k + 1, sl]
            a = acc * _sigmoid(acc)
            act_ref[:, sl] = a
            act_out_ref[0, rows, sl] = a.astype(act_out_ref.dtype)
        dt = _softplus(dt_ref[0, rows, :] + dtb_ref[...])

        def emit(j, y, rows=rows):
            sl = slice(j * _LANES, (j + 1) * _LANES)
            y_ref[0, rows, sl] = y + dskip_ref[:, sl] * act_ref[:, sl]

        _ssd_chunk(act_ref, dt, dt * a_neg, False, None, state_ref, xsc_ref, emit)


def _ssd_bwd_body(act_in_ref, dt_ref, dtb_ref, alog_ref, yf_ref, z_ref, ng_ref, epair_ref,
                  yn_ref, act_ref, state_ref, xsc_ref, ytot_ref):
    c = pl.program_id(1)

    @pl.when(c == 0)
    def _():
        state_ref[...] = jnp.zeros_like(state_ref)

    L = _CHUNK
    shift = _LANES - _SSD_HEADS
    dtb = pltpu.roll(dtb_ref[...], shift, 1)
    a_neg = -jnp.exp(pltpu.roll(alog_ref[...], shift, 1))
    gw = _SSD_INNER // _SSD_GROUPS
    for s in reversed(range(_SSD_STEP_CHUNKS)):
        rows = slice(s * L, (s + 1) * L)
        act_ref[...] = act_in_ref[0, rows, :].astype(_F32)
        dt = _softplus(pltpu.roll(dt_ref[0, rows, :], shift, 1) + dtb)

        def emit(j, y, rows=rows):
            sl = slice(j * _LANES, (j + 1) * _LANES)
            z = z_ref[0, rows, sl].astype(_F32)
            ytot_ref[:, sl] = (y + yf_ref[0, rows, sl]) * (z * _sigmoid(z))

        _ssd_chunk(act_ref, dt, dt * a_neg, True, epair_ref, state_ref, xsc_ref, emit)
        for g in range(_SSD_GROUPS):
            t = ytot_ref[:, g * gw:(g + 1) * gw]
            r = lax.rsqrt(jnp.mean(t * t, axis=-1, keepdims=True) + _EPS)
            yn_ref[0, rows, g * gw:(g + 1) * gw] = (t * r * ng_ref[:, g * gw:(g + 1) * gw]).astype(yn_ref.dtype)


def _spread_matrix():
    r = jnp.arange(3 * _LANES)[:, None] % _LANES
    return (r == jnp.arange(_SSD_INNER)[None, :] // _SSD_HEAD_DIM).astype(_BF16)


def _ssd_scratch():
    return [
        pltpu.VMEM((_CHUNK, _CONV_DIM), _F32),
        pltpu.VMEM((_SSD_GROUPS, _SSD_STATE, _SSD_INNER // _SSD_GROUPS), _F32),
        pltpu.VMEM((_CHUNK, _SSD_INNER), _BF16),
    ]


def _ssd_fwd(xbc, dt, conv_w, conv_b, dtb, alog, dskip, bsz, seq):
    nsteps = seq // _SSD_STEP
    hb = _SSD_STEP // _HALO
    xbc3 = xbc.reshape(bsz, seq, _CONV_DIM)
    dt3 = dt.reshape(bsz, seq, _DT_W)
    const = lambda shape: pl.BlockSpec(shape, lambda b, c: (0,) * len(shape))
    act, yf = pl.pallas_call(
        functools.partial(_ssd_fwd_body, nsteps=nsteps),
        grid=(bsz, nsteps),
        in_specs=[
            pl.BlockSpec((1, _HALO, _CONV_DIM), lambda b, c: (b, jnp.maximum(c * hb - 1, 0), 0)),
            pl.BlockSpec((1, _SSD_STEP, _CONV_DIM), lambda b, c: (b, c, 0)),
            pl.BlockSpec((1, _HALO, _CONV_DIM), lambda b, c: (b, jnp.minimum((c + 1) * hb, nsteps * hb - 1), 0)),
            pl.BlockSpec((1, _SSD_STEP, _DT_W), lambda b, c: (b, c, 0)),
            const((_CONV_W, _CONV_DIM)), const((1, _CONV_DIM)), const((1, _DT_W)), const((1, _DT_W)),
            const((1, _SSD_INNER)),
        ],
        out_specs=[
            pl.BlockSpec((1, _SSD_STEP, _CONV_DIM), lambda b, c: (b, c, 0)),
            pl.BlockSpec((1, _SSD_STEP, _SSD_INNER), lambda b, c: (b, c, 0)),
        ],
        out_shape=[
            jax.ShapeDtypeStruct((bsz, seq, _CONV_DIM), _BF16),
            jax.ShapeDtypeStruct((bsz, seq, _SSD_INNER), _F32),
        ],
        scratch_shapes=[pltpu.VMEM((_SSD_STEP + 2 * _HALO, _CONV_DIM), _F32)] + _ssd_scratch(),
        compiler_params=_cparams("parallel", "arbitrary"),
        name="ssd_fwd",
    )(xbc3, xbc3, xbc3, dt3, conv_w, conv_b, dtb, alog, dskip)
    return act, yf


def _ssd_bwd(act, dt, dtb, alog, yf, z, norm_g, bsz, seq):
    nsteps = seq // _SSD_STEP
    dt3 = dt.reshape(bsz, seq, _DT_W)
    z3 = z.reshape(bsz, seq, _SSD_INNER)
    epair = _spread_matrix()
    const = lambda shape: pl.BlockSpec(shape, lambda b, c: (0,) * len(shape))
    blk = lambda w: pl.BlockSpec((1, _SSD_STEP, w), lambda b, c: (b, nsteps - 1 - c, 0))
    return pl.pallas_call(
        _ssd_bwd_body,
        grid=(bsz, nsteps),
        in_specs=[blk(_CONV_DIM), blk(_DT_W), const((1, _DT_W)), const((1, _DT_W)),
                  blk(_SSD_INNER), blk(_SSD_INNER), const((1, _SSD_INNER)), const(epair.shape)],
        out_specs=blk(_SSD_INNER),
        out_shape=jax.ShapeDtypeStruct((bsz, seq, _SSD_INNER), _BF16),
        scratch_shapes=_ssd_scratch() + [pltpu.VMEM((_CHUNK, _SSD_INNER), _F32)],
        compiler_params=_cparams("parallel", "arbitrary"),
        name="ssd_bwd",
    )(act, dt3, dtb, alog, yf, z3, norm_g, epair)


_QB = 128


_QSUB = 2


def _attn_body(q_ref, kp_ref, kc_ref, kn_ref, vp_ref, vc_ref, vn_ref, o_ref, lse_ref, *, nstep):
    n = pl.program_id(2)
    ii = lax.broadcasted_iota(jnp.int32, (_QB, _QB), 0)
    jj = lax.broadcasted_iota(jnp.int32, (_QB, _QB), 1)
    dlt = jj - ii
    band = (dlt >= _QB - _ATT_RADIUS, jnp.abs(dlt) <= _ATT_RADIUS, dlt <= _ATT_RADIUS - _QB)
    lane_lo = lax.broadcasted_iota(jnp.int32, (1, _LANES), 1) < _ATT_HD
    nt = (((1,), (1,)), ((), ()))
    npair = _ATT_OUT // _LANES
    sl = [slice(p * _LANES, (p + 1) * _LANES) for p in range(npair)]

    def blocks(pr, cr, nr, p):
        return ([pr[0, 0, :, sl[p]]] + [cr[0, 0, s * _QB:(s + 1) * _QB, sl[p]] for s in range(_QSUB)]
                + [nr[0, 0, :, sl[p]]])

    ks = [blocks(kp_ref, kc_ref, kn_ref, p) for p in range(npair)]
    vs = [blocks(vp_ref, vc_ref, vn_ref, p) for p in range(npair)]
    chains = [(s, p, u) for s in range(_QSUB) for p in range(npair) for u in range(2)]
    scores = []
    for s, p, u in chains:
        qp = q_ref[0, 0, s * _QB:(s + 1) * _QB, sl[p]]
        hm = lane_lo if u == 0 else jnp.logical_not(lane_lo)
        qm = jnp.where(hm, qp, jnp.zeros_like(qp))
        masks = [band[0], band[1], band[2]]
        if s == 0:
            masks[0] = jnp.logical_and(band[0], n > 0)
        if s == _QSUB - 1:
            masks[2] = jnp.logical_and(band[2], n < nstep - 1)
        scores.append([jnp.where(m, lax.dot_general(qm, kx, nt, preferred_element_type=_F32), _NEG_BIG)
                       for m, kx in zip(masks, ks[p][s:s + 3])])
    mxs = [jnp.max(jnp.maximum(jnp.maximum(sc[0], sc[1]), sc[2]), axis=-1, keepdims=True) for sc in scores]
    probs = [[jnp.exp(sx - mx) for sx in sc] for sc, mx in zip(scores, mxs)]
    dens = [jnp.sum(pr[0] + pr[1] + pr[2], axis=-1, keepdims=True) for pr in probs]
    outs = []
    for (s, p, u), pr in zip(chains, probs):
        outs.append(sum(jnp.dot(px.astype(_BF16), vx, preferred_element_type=_F32)
                        for px, vx in zip(pr, vs[p][s:s + 3])))
    for c in range(0, len(chains), 2):
        s, p, _ = chains[c]
        rows = slice(s * _QB, (s + 1) * _QB)
        o0, o1 = outs[c] / dens[c], outs[c + 1] / dens[c + 1]
        l0 = jnp.broadcast_to(mxs[c] + jnp.log(dens[c]), (_QB, _LANES))
        l1 = jnp.broadcast_to(mxs[c + 1] + jnp.log(dens[c + 1]), (_QB, _LANES))
        o_ref[0, 0, rows, sl[p]] = jnp.where(lane_lo, o0, o1)
        lse_ref[0, 0, rows, sl[p]] = jnp.where(lane_lo, l0, l1)


def _attention_group(q, k, v, bsz, seq, dil):
    t = seq // dil
    nstep = t // (_QSUB * _QB)
    nblk = t // _QB
    own = pl.BlockSpec((1, 1, _QSUB * _QB, _ATT_OUT), lambda b, r, n: (b, r, n, 0))
    before = pl.BlockSpec((1, 1, _QB, _ATT_OUT), lambda b, r, n: (b, r, jnp.maximum(_QSUB * n - 1, 0), 0))
    after = pl.BlockSpec((1, 1, _QB, _ATT_OUT), lambda b, r, n: (b, r, jnp.minimum(_QSUB * (n + 1), nblk - 1), 0))
    return pl.pallas_call(
        functools.partial(_attn_body, nstep=nstep),
        grid=(bsz, dil, nstep),
        in_specs=[own, before, own, after, before, own, after],
        out_specs=[own, own],
        out_shape=[jax.ShapeDtypeStruct((bsz, dil, t, _ATT_OUT), _F32)] * 2,
        compiler_params=_cparams("parallel", "parallel", "parallel"),
        name=f"attn_d{dil}",
    )(q, k, k, k, v, v, v)


def _finish_body(yn_ref, o0_ref, l0_ref, o1_ref, l1_ref, o2_ref, l2_ref, gate_ref, x_ref,
                 wa_ref, wb_ref, wo_ref, ng_ref, wq_ref, x1_ref, h2_ref, qp_ref, mix_ref):
    tm = x_ref.shape[0]
    npair = _ATT_OUT // _LANES

    def token_order(ref, slot, dil):
        if dil == 1:
            return [ref[0, 0, :, p * _LANES:(p + 1) * _LANES] for p in range(npair)]
        for p in range(npair):
            for r in range(dil):
                mix_ref[slot * npair + p, pl.ds(r, tm // dil, stride=dil), :] = ref[0, r, :, p * _LANES:(p + 1) * _LANES]
        return [mix_ref[slot * npair + p] for p in range(npair)]

    dils = [d for _, d in _ATT_GROUPS]
    o_parts = [token_order(r, 2 * g, dils[g]) for g, r in enumerate((o0_ref, o1_ref, o2_ref))]
    l_parts = [token_order(r, 2 * g + 1, dils[g]) for g, r in enumerate((l0_ref, l1_ref, l2_ref))]
    halves = []
    for p in range(npair):
        l0, l1, l2 = l_parts[0][p], l_parts[1][p], l_parts[2][p]
        lm = jnp.maximum(jnp.maximum(l0, l1), l2)
        e0, e1, e2 = jnp.exp(l0 - lm), jnp.exp(l1 - lm), jnp.exp(l2 - lm)
        den = e0 + e1 + e2
        halves.append((e0 / den) * o_parts[0][p] + (e1 / den) * o_parts[1][p] + (e2 / den) * o_parts[2][p])
    o = jnp.concatenate(halves, axis=1)
    a = jnp.dot(yn_ref[...], wa_ref[...], preferred_element_type=_F32)
    b = jnp.dot(o.astype(_BF16), wb_ref[...], preferred_element_type=_F32)
    ga = _sigmoid(gate_ref[:, 0:_D].astype(_F32))
    gb = _sigmoid(gate_ref[:, _D:2 * _D].astype(_F32))
    merged = (ga * a + gb * b).astype(_BF16)
    x1 = x_ref[...] + jnp.dot(merged, wo_ref[...], preferred_element_type=_F32)
    x1_ref[...] = x1
    y = x1 * lax.rsqrt(jnp.mean(x1 * x1, axis=-1, keepdims=True) + _EPS)
    h2 = (y * ng_ref[...]).astype(_BF16)
    h2_ref[...] = h2
    for head in range(_PEER_HEADS):
        sl = slice(2 * head * _PEER_KEYS, 2 * (head + 1) * _PEER_KEYS)
        q2 = jnp.dot(h2, wq_ref[:, sl], preferred_element_type=_F32).astype(qp_ref.dtype)
        qp_ref[2 * head] = q2[:, 0:_PEER_KEYS]
        qp_ref[2 * head + 1] = q2[:, _PEER_KEYS:2 * _PEER_KEYS]


def _finish(yn, attn, gates, x2d, wa, wb, wo, ng, wq, seq, tm):
    n = x2d.shape[0]
    nseq = seq // tm
    row = lambda w: pl.BlockSpec((tm, w), lambda i: (i, 0))
    const = lambda shape: pl.BlockSpec(shape, lambda i: (0, 0), pipeline_mode=pl.Buffered(1))
    cls = lambda dil: pl.BlockSpec((1, dil, tm // dil, _ATT_OUT), lambda i: (i // nseq, 0, i % nseq, 0))
    attn_specs = [cls(dil) for _, dil in _ATT_GROUPS for _ in range(2)]
    nq = 2 * _PEER_HEADS
    return pl.pallas_call(
        _finish_body,
        grid=(n // tm,),
        in_specs=[row(_SSD_INNER)] + attn_specs + [row(_GATE_W), row(_D),
                  const((_SSD_INNER, _D)), const((_ATT_OUT, _D)), const((_D, _D)), const((1, _D)),
                  const((_D, nq * _PEER_KEYS))],
        out_specs=[row(_D), row(_D), pl.BlockSpec((nq, tm, _PEER_KEYS), lambda i: (0, i, 0))],
        out_shape=[jax.ShapeDtypeStruct((n, _D), _F32), jax.ShapeDtypeStruct((n, _D), _BF16),
                   jax.ShapeDtypeStruct((nq, n, _PEER_KEYS), _BF16)],
        scratch_shapes=[pltpu.VMEM((2 * len(_ATT_GROUPS) * (_ATT_OUT // _LANES), tm, _LANES), _F32)],
        compiler_params=_cparams("parallel"),
        name="finish",
    )(yn, *attn, gates, x2d, wa, wb, wo, ng, wq)


_ROUTE_TB = 1024
_ROUTE_SUB = _ROUTE_TB // _LANES
_ROUTE_CHAINS = 4


def _sweep_top(ref, nrows, k, codes_ref=None):
    sub = _ROUTE_SUB
    neg = jnp.full((sub, _LANES), -jnp.inf, _F32)
    vals, outs = [], []
    prev = None
    for r in range(k):
        best = [None] * _ROUTE_CHAINS
        for row in range(nrows):
            v = ref[row * sub:(row + 1) * sub, :]
            if prev is not None:
                v = jnp.where(prev == row, neg, v)
                ref[row * sub:(row + 1) * sub, :] = v
            c = row % _ROUTE_CHAINS
            rid = jnp.full((sub, _LANES), row, jnp.int32)
            code = codes_ref[row * sub:(row + 1) * sub, :] if codes_ref is not None else None
            if best[c] is None:
                best[c] = (v, rid, code)
            else:
                bv, bk, bc = best[c]
                better = v > bv
                best[c] = (jnp.maximum(bv, v), jnp.where(better, rid, bk),
                           jnp.where(better, code, bc) if code is not None else None)
        acc = best[0]
        for c in range(1, min(_ROUTE_CHAINS, nrows)):
            av, ak, ac = acc
            bv, bk, bc = best[c]
            take = jnp.logical_or(bv > av, jnp.logical_and(bv == av, bk < ak))
            acc = (jnp.maximum(av, bv), jnp.where(take, bk, ak), jnp.where(take, bc, ac) if ac is not None else None)
        vals.append(acc[0])
        outs.append(acc[2] if codes_ref is not None else acc[1])
        prev = acc[1]
    return vals, outs


def _route_body(qp_ref, keys_ref, i_ref, j_ref, g_ref, s_ref, cand_ref, code_ref, is_ref, js_ref, gs_ref):
    nt = (((1,), (1,)), ((), ()))
    k = _PEER_TOPK
    sub = _ROUTE_SUB
    pairs = [(a, b) for a in range(k) for b in range(k // (a + 1))]

    def head(h, carry):
        hv, hi = [], []
        for c in range(2):
            for a in range(sub):
                tile = lax.dot_general(keys_ref[2 * h + c], qp_ref[2 * h + c, a * _LANES:(a + 1) * _LANES, :], nt,
                                       preferred_element_type=_F32)
                s_ref[c, pl.ds(a, _PEER_KEYS, stride=sub), :] = tile
            v, i = _sweep_top(s_ref.at[c], _PEER_KEYS, k)
            hv.append(v)
            hi.append(i)
        for p, (a, b) in enumerate(pairs):
            cand_ref[p * sub:(p + 1) * sub, :] = hv[0][a] + hv[1][b]
            code_ref[p * sub:(p + 1) * sub, :] = hi[0][a] * _PEER_KEYS + hi[1][b]
        fs, codes = _sweep_top(cand_ref, len(pairs), k, codes_ref=code_ref)
        e = [jnp.exp(f - fs[0]) for f in fs]
        den = e[0]
        for x in e[1:]:
            den = den + x
        for r in range(k):
            r0 = pl.multiple_of((h * k + r) * sub, sub)
            is_ref[pl.ds(r0, sub), :] = jnp.right_shift(codes[r], 7).astype(_F32)
            js_ref[pl.ds(r0, sub), :] = jnp.bitwise_and(codes[r], _PEER_KEYS - 1).astype(_F32)
            gs_ref[pl.ds(r0, sub), :] = e[r] / den
        return carry

    lax.fori_loop(0, _PEER_HEADS, head, 0)
    slots = _PEER_HEADS * k
    for src, dst in ((is_ref, i_ref), (js_ref, j_ref), (gs_ref, g_ref)):
        for a in range(sub):
            dst[a * _LANES:(a + 1) * _LANES, :] = src[pl.ds(a, slots, stride=sub), :].T


def _peer_route(qp, sub_keys16):
    nq, n, _ = qp.shape
    tq = _ROUTE_TB
    slots = _PEER_HEADS * _PEER_TOPK
    ncand = sum(_PEER_TOPK // (a + 1) for a in range(_PEER_TOPK))
    out = pl.BlockSpec((tq, slots), lambda i: (i, 0))
    return pl.pallas_call(
        _route_body,
        grid=(n // tq,),
        in_specs=[pl.BlockSpec((nq, tq, _PEER_KEYS), lambda i: (0, i, 0)),
                  pl.BlockSpec((nq, _PEER_KEYS, _PEER_KEYS), lambda i: (0, 0, 0))],
        out_specs=[out] * 3,
        out_shape=[jax.ShapeDtypeStruct((n, slots), _F32)] * 3,
        scratch_shapes=[pltpu.VMEM((2, _PEER_KEYS * _ROUTE_SUB, _LANES), _F32),
                        pltpu.VMEM((ncand * _ROUTE_SUB, _LANES), _F32),
                        pltpu.VMEM((ncand * _ROUTE_SUB, _LANES), jnp.int32)]
        + [pltpu.VMEM((slots * _ROUTE_SUB, _LANES), _F32)] * 3,
        compiler_params=_cparams("parallel"),
        name="peer_route",
    )(qp, sub_keys16)


_PEER_TM = 512
_PEER_HALF = _PEER_TM // 2
_PEER_EB = 2048
_PEER_CW = 256
_PEER_PITCH = _PEER_HALF + 8
_PEER_UNROLL = 32


def _peer_body(h_ref, i_ref, j_ref, g_ref, x1_ref, u_ref, v_ref, o_ref, gate_ref, w_ref):
    e = pl.program_id(1)
    slots = _PEER_HEADS * _PEER_TOPK
    nt = (((1,), (1,)), ((), ()))
    hi_mask = jnp.uint32(0xFFFF0000)

    @pl.when(e == 0)
    def _():
        o_ref[...] = x1_ref[...]
        sub = lax.broadcasted_iota(jnp.int32, (_PEER_KEYS, slots), 0).astype(_F32).astype(_BF16)
        one = jnp.ones((), _BF16)
        zero = jnp.zeros((), _BF16)

        def gate_bits(n):
            irow = i_ref[pl.ds(n, 1), :].astype(_BF16)
            jrow = j_ref[pl.ds(n, 1), :].astype(_BF16)
            grow = g_ref[pl.ds(n, 1), :].astype(_BF16)
            p_t = jnp.where(sub == irow, one, zero)
            q_t = jnp.where(sub == jrow, grow, zero)
            gm = lax.dot_general(p_t, q_t, nt, preferred_element_type=_F32)
            return pltpu.bitcast(gm, jnp.uint32)

        def pair(n, carry):
            lo = jnp.right_shift(gate_bits(n), jnp.uint32(16))
            hi = jnp.bitwise_and(gate_bits(n + _PEER_HALF), hi_mask)
            gate_ref[pl.ds(n, _PEER_KEYS, stride=_PEER_PITCH), :] = jnp.bitwise_or(lo, hi)
            return carry

        lax.fori_loop(0, _PEER_HALF, pair, 0, unroll=_PEER_UNROLL)

    rows_per_chunk = _PEER_CW // _PEER_KEYS
    h = h_ref[...]
    for c in range(_PEER_EB // _PEER_CW):
        hmat = lax.dot_general(h, u_ref[c * _PEER_CW:(c + 1) * _PEER_CW, :], nt, preferred_element_type=_F32)
        act = 0.5 * hmat * (1.0 + lax.erf(hmat * (1.0 / math.sqrt(2.0))))
        cols = []
        for r in range(rows_per_chunk):
            row = e * (_PEER_EB // _PEER_KEYS) + c * rows_per_chunk + r
            word = gate_ref[pl.ds(pl.multiple_of(row * _PEER_PITCH, 8), _PEER_HALF), :]
            lo = pltpu.bitcast(jnp.left_shift(word, jnp.uint32(16)), _F32)
            hi = pltpu.bitcast(jnp.bitwise_and(word, hi_mask), _F32)
            cols.append(jnp.concatenate([lo, hi], axis=0))
        gate = jnp.concatenate(cols, axis=1)
        w_ref[:, c * _PEER_CW:(c + 1) * _PEER_CW] = (gate * act).astype(_BF16)
    o_ref[...] += jnp.dot(w_ref[...], v_ref[...], preferred_element_type=_F32)


def _peer_dense(h2, ri, rj, rg, x1, u16, v16):
    n = h2.shape[0]
    tm, eb = _PEER_TM, _PEER_EB
    slots = _PEER_HEADS * _PEER_TOPK
    nexp = u16.shape[0]
    tok = lambda w: pl.BlockSpec((tm, w), lambda i, e: (i, 0))
    return pl.pallas_call(
        _peer_body,
        grid=(n // tm, nexp // eb),
        in_specs=[tok(_D), tok(slots), tok(slots), tok(slots), tok(_D),
                  pl.BlockSpec((eb, _D), lambda i, e: (e, 0)),
                  pl.BlockSpec((eb, _D), lambda i, e: (e, 0))],
        out_specs=tok(_D),
        out_shape=jax.ShapeDtypeStruct((n, _D), _F32),
        scratch_shapes=[pltpu.VMEM((_PEER_KEYS * _PEER_PITCH, _PEER_KEYS), jnp.uint32),
                        pltpu.VMEM((tm, eb), _BF16)],
        compiler_params=_cparams("parallel", "arbitrary"),
        name="peer_dense",
    )(h2, ri, rj, rg, x1, u16, v16)


def _prepare(norm_mix_g, w_in, conv_w, conv_b, dt_bias, a_log, d_skip, ssd_norm_g, w_a,
             q_norm_g, k_norm_g, w_b, w_o, norm_ffn_g, w_query, sub_keys, expert_u, expert_v):
    xbc_end = _SSD_INNER + _CONV_DIM
    dt_end = xbc_end + 2 * _SSD_HEADS
    pad_dt = lambda a: jnp.pad(a.astype(_F32).reshape(1, 2 * _SSD_HEADS), ((0, 0), (0, _DT_W - 2 * _SSD_HEADS)))
    row = lambda a: a.astype(_F32)[None, :]
    return dict(
        w_perm=jnp.concatenate(
            [w_in[:, :xbc_end], w_in[:, dt_end:], w_in[:, xbc_end:dt_end],
             jnp.zeros((_D, _DT_W - 2 * _SSD_HEADS), w_in.dtype)], axis=1).astype(_BF16),
        norm_mix_g=row(norm_mix_g), conv_w=conv_w.astype(_F32), conv_b=row(conv_b),
        dt_bias=pad_dt(dt_bias), a_log=pad_dt(a_log), d_skip=jnp.repeat(d_skip.astype(_F32), _SSD_HEAD_DIM)[None, :],
        ssd_norm_g=row(ssd_norm_g), q_norm_g=q_norm_g, k_norm_g=k_norm_g,
        w_a=w_a.astype(_BF16), w_b=w_b.astype(_BF16), w_o=w_o.astype(_BF16), norm_ffn_g=row(norm_ffn_g),
        w_query=w_query.astype(_BF16),
        keys16=sub_keys.reshape(2 * _PEER_HEADS, _PEER_KEYS, sub_keys.shape[-1]).astype(_BF16),
        u16=expert_u.astype(_BF16), v16=expert_v.astype(_BF16),
    )


def _layer(x, w):
    bsz, seq, d_model = x.shape
    n = bsz * seq
    max_dil = max(d for _, d in _ATT_GROUPS)
    assert d_model == _D and seq % _TOKEN_TILE == 0 and seq % _SSD_STEP == 0, x.shape
    assert seq % (max_dil * _QSUB * _QB) == 0 and _TOKEN_TILE % (max_dil * _HALO) == 0, x.shape
    assert n % _ROUTE_TB == 0 and n % _PEER_TM == 0, x.shape
    x2d = x.reshape(n, _D)
    z, xbc, qkv, gates, dt = _in_proj(x2d, w["norm_mix_g"], w["w_perm"], tm=_TOKEN_TILE)
    qkv_groups = _qk_prep(qkv, w["q_norm_g"], w["k_norm_g"], bsz, seq, tm=_TOKEN_TILE)
    act, yf = _ssd_fwd(xbc, dt, w["conv_w"], w["conv_b"], w["dt_bias"], w["a_log"], w["d_skip"], bsz, seq)
    yn = _ssd_bwd(act, dt, w["dt_bias"], w["a_log"], yf, z, w["ssd_norm_g"], bsz, seq)
    attn = []
    for gi, (window, dil) in enumerate(_ATT_GROUPS):
        assert window // (2 * dil) == _ATT_RADIUS
        attn.extend(_attention_group(*qkv_groups[3 * gi:3 * gi + 3], bsz, seq, dil))
    x1, h2, qp = _finish(yn.reshape(n, _SSD_INNER), attn, gates, x2d, w["w_a"], w["w_b"], w["w_o"],
                         w["norm_ffn_g"], w["w_query"], seq, tm=_TOKEN_TILE)
    ri, rj, rg = _peer_route(qp, w["keys16"])
    out = _peer_dense(h2, ri, rj, rg, x1, w["u16"], w["v16"])
    return out.reshape(bsz, seq, _D)


def kernel(x_prompt, x_sample, norm_mix_g, w_in, conv_w, conv_b, dt_bias, a_log, d_skip, ssd_norm_g, w_a,
           q_norm_g, k_norm_g, w_b, w_o, norm_ffn_g, w_query, sub_keys, expert_u, expert_v):
    weights = (norm_mix_g, w_in, conv_w, conv_b, dt_bias, a_log, d_skip, ssd_norm_g, w_a,
               q_norm_g, k_norm_g, w_b, w_o, norm_ffn_g, w_query, sub_keys, expert_u, expert_v)
    outs = [x_prompt, x_sample]
    for l in range(w_in.shape[0]):
        w = _prepare(*(a[l] for a in weights))
        outs = [_layer(x, w) for x in outs]
    return tuple(outs)
```

```python
import functools
import math

import jax
import jax.numpy as jnp
from jax import lax
from jax.experimental import pallas as pl
from jax.experimental.pallas import tpu as pltpu

_F32 = jnp.float32
_BF16 = jnp.bfloat16
_HIGHEST = lax.Precision.HIGHEST

_V7X_VMEM_BYTES = 64 * 1024 * 1024
_VMEM_LIMIT = _V7X_VMEM_BYTES - 8 * 1024 * 1024
_LANES = 128
_TOKEN_TILE = 512

_EPS = 1e-6
_D = 1024
_SSD_HEADS = 32
_SSD_HEAD_DIM = 64
_SSD_INNER = 2048
_SSD_GROUPS = 4
_SSD_STATE = 128
_SSD_BC = 512
_CONV_W = 5
_CONV_DIM = 3072
_CHUNK = 128
_HALO = 16
_ATT_GROUPS = ((128, 1), (512, 4), (2048, 16))
_ATT_WIDTH = 768
_ATT_OUT = 256
_ATT_HD = 64
_ATT_RADIUS = 64
_ROPE_DIMS = 16
_ROPE_THETA = 500000.0
_NEG_BIG = -1e30
_PEER_HEADS = 8
_PEER_KEYS = 128
_PEER_TOPK = 16
_QKV_W = 3 * _ATT_WIDTH
_GATE_W = 2 * _D
_DT_W = 128
_IN_COLS_PAD = _SSD_INNER + _CONV_DIM + _QKV_W + _GATE_W + _DT_W


def _cparams(*sem):
    return pltpu.CompilerParams(dimension_semantics=sem, vmem_limit_bytes=_VMEM_LIMIT)


def _sigmoid(x):
    return 1.0 / (1.0 + jnp.exp(-x))


_COL_CHUNK = 512


def _inproj_body(x_ref, g_ref, w_ref, z_ref, xbc_ref, qkv_ref, gate_ref, dt_ref):
    x = x_ref[...]
    y = x * lax.rsqrt(jnp.mean(x * x, axis=-1, keepdims=True) + _EPS)
    h = (y * g_ref[...]).astype(_BF16)
    col = 0
    for ref in (z_ref, xbc_ref, qkv_ref, gate_ref, dt_ref):
        width = ref.shape[-1]
        for c0 in range(0, width, _COL_CHUNK):
            cw = min(_COL_CHUNK, width - c0)
            acc = jnp.dot(h, w_ref[:, col + c0:col + c0 + cw], preferred_element_type=_F32)
            ref[:, c0:c0 + cw] = acc.astype(ref.dtype)
        col += width


def _in_proj(x2d, g, w_perm, tm):
    n = x2d.shape[0]
    widths = (_SSD_INNER, _CONV_DIM, _QKV_W, _GATE_W, _DT_W)
    dtypes = (_BF16, _BF16, _BF16, _BF16, _F32)
    return pl.pallas_call(
        _inproj_body,
        grid=(n // tm,),
        in_specs=[
            pl.BlockSpec((tm, _D), lambda i: (i, 0)),
            pl.BlockSpec((1, _D), lambda i: (0, 0)),
            pl.BlockSpec((_D, _IN_COLS_PAD), lambda i: (0, 0), pipeline_mode=pl.Buffered(1)),
        ],
        out_specs=[pl.BlockSpec((tm, w), lambda i: (i, 0)) for w in widths],
        out_shape=[jax.ShapeDtypeStruct((n, w), dt) for w, dt in zip(widths, dtypes)],
        compiler_params=_cparams("parallel"),
        name="in_proj",
    )(x2d, g, w_perm)


def _qkprep_body(q_ref, k_ref, v_ref, qg_ref, kg_ref, c_ref, s1_ref, s2_ref, bd_ref, *refs):
    outs, tmp_ref = refs[:-1], refs[-1]
    tm = q_ref.shape[0]
    cosv = c_ref[...]
    s1 = s1_ref[...]
    s2 = s2_ref[...]
    bd = bd_ref[...]
    cols_per_group = _ATT_OUT // _LANES
    for ti, (src, g_ref, scale) in enumerate(((q_ref, qg_ref, _ATT_HD ** -0.5), (k_ref, kg_ref, 1.0), (v_ref, None, 1.0))):
        for p in range(_ATT_WIDTH // _LANES):
            x = src[:, p * _LANES:(p + 1) * _LANES].astype(_F32)
            if g_ref is not None:
                ss = jnp.dot(_split3(x * x), bd, preferred_element_type=_F32)
                y = x * lax.rsqrt(ss * (1.0 / _ATT_HD) + _EPS) * g_ref[...]
                x = y * cosv + pltpu.roll(y, 8, 1) * s1 + pltpu.roll(y, _LANES - 8, 1) * s2
                if scale != 1.0:
                    x = x * scale
            gi, pl_ = divmod(p, cols_per_group)
            dil = _ATT_GROUPS[gi][1]
            dst = outs[3 * gi + ti]
            lanes = slice(pl_ * _LANES, (pl_ + 1) * _LANES)
            if dil == 1:
                dst[0, 0, :, lanes] = x.astype(dst.dtype)
            else:
                tmp_ref[...] = x
                for r in range(dil):
                    dst[0, r, :, lanes] = tmp_ref[pl.ds(r, tm // dil, stride=dil), :].astype(dst.dtype)


def _rope_tables(seq):
    half = _ROPE_DIMS // 2
    inv = _ROPE_THETA ** (-jnp.arange(half, dtype=_F32) * 2.0 / _ROPE_DIMS)
    ang = jnp.arange(seq, dtype=_F32)[:, None] * inv[None, :]
    cos, sin = jnp.cos(ang), jnp.sin(ang)
    ones = jnp.ones((seq, _ATT_HD - _ROPE_DIMS), _F32)
    zeros = jnp.zeros((seq, _ATT_HD - _ROPE_DIMS), _F32)
    zh = jnp.zeros((seq, half), _F32)
    c = jnp.concatenate([cos, cos, ones], axis=1)
    s1 = jnp.concatenate([zh, sin, zeros], axis=1)
    s2 = jnp.concatenate([-sin, zh, zeros], axis=1)
    rep = _LANES // _ATT_HD
    return tuple(jnp.tile(t, (1, rep)) for t in (c, s1, s2))


def _qk_prep(qkv, q_g, k_g, bsz, seq, tm):
    c, s1, s2 = _rope_tables(seq)
    nseq = seq // tm
    lane = jnp.arange(_LANES)
    bd = jnp.tile((lane[:, None] // _ATT_HD == lane[None, :] // _ATT_HD).astype(_BF16), (3, 1))
    rep = _LANES // _ATT_HD
    qg = jnp.tile(q_g.astype(_F32), rep)[None, :]
    kg = jnp.tile(k_g.astype(_F32), rep)[None, :]
    tab = pl.BlockSpec((tm, _LANES), lambda b, i: (i, 0))
    const = lambda shape: pl.BlockSpec(shape, lambda b, i: (0, 0))
    col = lambda cblk: pl.BlockSpec((tm, _ATT_WIDTH), lambda b, i: (b * nseq + i, cblk))
    out_specs, out_shapes = [], []
    for _, dil in _ATT_GROUPS:
        for _ in range(3):
            out_specs.append(pl.BlockSpec((1, dil, tm // dil, _ATT_OUT), lambda b, i: (b, 0, i, 0)))
            out_shapes.append(jax.ShapeDtypeStruct((bsz, dil, seq // dil, _ATT_OUT), _BF16))
    return pl.pallas_call(
        _qkprep_body,
        grid=(bsz, nseq),
        in_specs=[col(0), col(1), col(2), const((1, _LANES)), const((1, _LANES)), tab, tab, tab,
                  const((3 * _LANES, _LANES))],
        out_specs=out_specs,
        out_shape=out_shapes,
        scratch_shapes=[pltpu.VMEM((tm, _LANES), _F32)],
        compiler_params=_cparams("parallel", "parallel"),
        name="qk_prep",
    )(qkv, qkv, qkv, qg, kg, c, s1, s2, bd)


def _pair_bcast(a, lane_lo, h):
    rows = a.shape[0]
    lo = jnp.broadcast_to(a[:, h:h + 1], (rows, _LANES))
    hi = jnp.broadcast_to(a[:, h + 1:h + 2], (rows, _LANES))
    return jnp.where(lane_lo, lo, hi)


def _split3(x):
    hi = x.astype(_BF16)
    r1 = x - hi.astype(_F32)
    mid = r1.astype(_BF16)
    lo = (r1 - mid.astype(_F32)).astype(_BF16)
    return jnp.concatenate([hi, mid, lo], axis=1)


def _ssd_chunk(act_ref, dt, da, rev, epair_ref, state_ref, xsc_ref, emit):
    L = _CHUNK
    row = lax.broadcasted_iota(jnp.int32, (L, L), 0)
    col = lax.broadcasted_iota(jnp.int32, (L, L), 1)
    keep = (row <= col) if rev else (row >= col)
    keep_t = (row >= col) if rev else (row <= col)
    cs = jnp.dot(keep.astype(_F32), da, precision=_HIGHEST, preferred_element_type=_F32)
    cs_t = jnp.dot(da.T, keep_t.astype(_F32), precision=_HIGHEST, preferred_element_type=_F32)
    dt_t = dt.T
    cs_end = cs[0:1, :] if rev else cs[L - 1:L, :]
    cdec = jnp.exp(cs_end)
    ecs = jnp.exp(cs)
    wst = jnp.exp(cs_end - cs) * dt
    lane_lo = lax.broadcasted_iota(jnp.int32, (1, _LANES), 1) < _SSD_HEAD_DIM
    if epair_ref is not None:
        ecs_all = jnp.dot(_split3(ecs), epair_ref[...], preferred_element_type=_F32)
        wst_all = jnp.dot(_split3(wst), epair_ref[...], preferred_element_type=_F32)
        xsc_ref[...] = (act_ref[:, 0:_SSD_INNER] * wst_all).astype(_BF16)
    heads_per_group = _SSD_HEADS // _SSD_GROUPS
    pairs_per_group = heads_per_group // 2
    gw = heads_per_group * _SSD_HEAD_DIM
    for g in range(_SSD_GROUPS):
        bm = act_ref[:, _SSD_INNER + g * _SSD_STATE:_SSD_INNER + (g + 1) * _SSD_STATE]
        cm = act_ref[:, _SSD_INNER + _SSD_BC + g * _SSD_STATE:_SSD_INNER + _SSD_BC + (g + 1) * _SSD_STATE]
        bm16 = bm.astype(_BF16)
        cm16 = cm.astype(_BF16)
        cb = lax.dot_general(cm16, bm16, (((1,), (1,)), ((), ())), preferred_element_type=_F32)
        y_off = jnp.dot(cm16, state_ref[g].astype(_BF16), preferred_element_type=_F32)
        cdec_cols = []
        for jj in range(pairs_per_group):
            j = g * pairs_per_group + jj
            sl = slice(j * _LANES, (j + 1) * _LANES)
            xs = act_ref[:, sl]
            xs16 = xs.astype(_BF16)
            res = []
            for u in range(2):
                h = 2 * j + u
                diff = jnp.broadcast_to(cs[:, h:h + 1], (L, L)) - cs_t[h:h + 1, :]
                decay = jnp.exp(jnp.where(keep, diff, _NEG_BIG))
                m = (cb * decay * dt_t[h:h + 1, :]).astype(_BF16)
                res.append(jnp.dot(m, xs16, preferred_element_type=_F32))
            y = jnp.where(lane_lo, res[0], res[1])
            if epair_ref is not None:
                ec = ecs_all[:, sl]
            else:
                ec = _pair_bcast(ecs, lane_lo, 2 * j)
                xsc_ref[:, sl] = (xs * _pair_bcast(wst, lane_lo, 2 * j)).astype(_BF16)
            emit(j, y + y_off[:, jj * _LANES:(jj + 1) * _LANES] * ec)
            cdec_cols.append(_pair_bcast(cdec, lane_lo, 2 * j))
        new = jnp.dot(bm.T.astype(_BF16), xsc_ref[:, g * gw:(g + 1) * gw], preferred_element_type=_F32)
        state_ref[g] = state_ref[g] * jnp.concatenate(cdec_cols, axis=1) + new


def _softplus(x):
    return jnp.maximum(x, 0.0) + jnp.log(1.0 + jnp.exp(-jnp.abs(x)))


_SSD_STEP_CHUNKS = 2
_SSD_STEP = _SSD_STEP_CHUNKS * _CHUNK


def _ssd_fwd_body(xp_ref, xc_ref, xn_ref, dt_ref, cw_ref, cb_ref, dtb_ref, alog_ref, dskip_ref,
                  act_out_ref, y_ref, win_ref, act_ref, state_ref, xsc_ref, *, nsteps):
    c = pl.program_id(1)

    @pl.when(c == 0)
    def _():
        state_ref[...] = jnp.zeros_like(state_ref)

    L = _CHUNK
    prev_ok = (c > 0).astype(_F32)
    next_ok = (c < nsteps - 1).astype(_F32)
    win_ref[0:_HALO, :] = xp_ref[0].astype(_F32) * prev_ok
    win_ref[_HALO:_HALO + _SSD_STEP, :] = xc_ref[0].astype(_F32)
    win_ref[_HALO + _SSD_STEP:2 * _HALO + _SSD_STEP, :] = xn_ref[0].astype(_F32) * next_ok
    pad = (_CONV_W - 1) // 2
    cw = cw_ref[...]
    a_neg = -jnp.exp(alog_ref[...])
    for s in range(_SSD_STEP_CHUNKS):
        rows = slice(s * L, (s + 1) * L)
        for c0 in range(0, _CONV_DIM, _COL_CHUNK):
            sl = slice(c0, c0 + _COL_CHUNK)
            acc = jnp.broadcast_to(cb_ref[:, sl], (L, _COL_CHUNK))
            for k in range(_CONV_W):
                r0 = _HALO + s * L - pad + k
                acc = acc + win_ref[r0:r0 + L, sl] * cw[k:k + 1, sl]
            a = acc * _sigmoid(acc)
            act_ref[:, sl] = a
            act_out_ref[0, rows, sl] = a.astype(act_out_ref.dtype)
        dt = _softplus(dt_ref[0, rows, :] + dtb_ref[...])

        def emit(j, y, rows=rows):
            sl = slice(j * _LANES, (j + 1) * _LANES)
            y_ref[0, rows, sl] = y + dskip_ref[:, sl] * act_ref[:, sl]

        _ssd_chunk(act_ref, dt, dt * a_neg, False, None, state_ref, xsc_ref, emit)


def _ssd_bwd_body(act_in_ref, dt_ref, dtb_ref, alog_ref, yf_ref, z_ref, ng_ref, epair_ref,
                  yn_ref, act_ref, state_ref, xsc_ref, ytot_ref):
    c = pl.program_id(1)

    @pl.when(c == 0)
    def _():
        state_ref[...] = jnp.zeros_like(state_ref)

    L = _CHUNK
    shift = _LANES - _SSD_HEADS
    dtb = pltpu.roll(dtb_ref[...], shift, 1)
    a_neg = -jnp.exp(pltpu.roll(alog_ref[...], shift, 1))
    gw = _SSD_INNER // _SSD_GROUPS
    for s in reversed(range(_SSD_STEP_CHUNKS)):
        rows = slice(s * L, (s + 1) * L)
        act_ref[...] = act_in_ref[0, rows, :].astype(_F32)
        dt = _softplus(pltpu.roll(dt_ref[0, rows, :], shift, 1) + dtb)

        def emit(j, y, rows=rows):
            sl = slice(j * _LANES, (j + 1) * _LANES)
            z = z_ref[0, rows, sl].astype(_F32)
            ytot_ref[:, sl] = (y + yf_ref[0, rows, sl]) * (z * _sigmoid(z))

        _ssd_chunk(act_ref, dt, dt * a_neg, True, epair_ref, state_ref, xsc_ref, emit)
        for g in range(_SSD_GROUPS):
            t = ytot_ref[:, g * gw:(g + 1) * gw]
            r = lax.rsqrt(jnp.mean(t * t, axis=-1, keepdims=True) + _EPS)
            yn_ref[0, rows, g * gw:(g + 1) * gw] = (t * r * ng_ref[:, g * gw:(g + 1) * gw]).astype(yn_ref.dtype)


def _spread_matrix():
    r = jnp.arange(3 * _LANES)[:, None] % _LANES
    return (r == jnp.arange(_SSD_INNER)[None, :] // _SSD_HEAD_DIM).astype(_BF16)


def _ssd_scratch():
    return [
        pltpu.VMEM((_CHUNK, _CONV_DIM), _F32),
        pltpu.VMEM((_SSD_GROUPS, _SSD_STATE, _SSD_INNER // _SSD_GROUPS), _F32),
        pltpu.VMEM((_CHUNK, _SSD_INNER), _BF16),
    ]


def _ssd_fwd(xbc, dt, conv_w, conv_b, dtb, alog, dskip, bsz, seq):
    nsteps = seq // _SSD_STEP
    hb = _SSD_STEP // _HALO
    xbc3 = xbc.reshape(bsz, seq, _CONV_DIM)
    dt3 = dt.reshape(bsz, seq, _DT_W)
    const = lambda shape: pl.BlockSpec(shape, lambda b, c: (0,) * len(shape))
    act, yf = pl.pallas_call(
        functools.partial(_ssd_fwd_body, nsteps=nsteps),
        grid=(bsz, nsteps),
        in_specs=[
            pl.BlockSpec((1, _HALO, _CONV_DIM), lambda b, c: (b, jnp.maximum(c * hb - 1, 0), 0)),
            pl.BlockSpec((1, _SSD_STEP, _CONV_DIM), lambda b, c: (b, c, 0)),
            pl.BlockSpec((1, _HALO, _CONV_DIM), lambda b, c: (b, jnp.minimum((c + 1) * hb, nsteps * hb - 1), 0)),
            pl.BlockSpec((1, _SSD_STEP, _DT_W), lambda b, c: (b, c, 0)),
            const((_CONV_W, _CONV_DIM)), const((1, _CONV_DIM)), const((1, _DT_W)), const((1, _DT_W)),
            const((1, _SSD_INNER)),
        ],
        out_specs=[
            pl.BlockSpec((1, _SSD_STEP, _CONV_DIM), lambda b, c: (b, c, 0)),
            pl.BlockSpec((1, _SSD_STEP, _SSD_INNER), lambda b, c: (b, c, 0)),
        ],
        out_shape=[
            jax.ShapeDtypeStruct((bsz, seq, _CONV_DIM), _BF16),
            jax.ShapeDtypeStruct((bsz, seq, _SSD_INNER), _F32),
        ],
        scratch_shapes=[pltpu.VMEM((_SSD_STEP + 2 * _HALO, _CONV_DIM), _F32)] + _ssd_scratch(),
        compiler_params=_cparams("parallel", "arbitrary"),
        name="ssd_fwd",
    )(xbc3, xbc3, xbc3, dt3, conv_w, conv_b, dtb, alog, dskip)
    return act, yf


def _ssd_bwd(act, dt, dtb, alog, yf, z, norm_g, bsz, seq):
    nsteps = seq // _SSD_STEP
    dt3 = dt.reshape(bsz, seq, _DT_W)
    z3 = z.reshape(bsz, seq, _SSD_INNER)
    epair = _spread_matrix()
    const = lambda shape: pl.BlockSpec(shape, lambda b, c: (0,) * len(shape))
    blk = lambda w: pl.BlockSpec((1, _SSD_STEP, w), lambda b, c: (b, nsteps - 1 - c, 0))
    return pl.pallas_call(
        _ssd_bwd_body,
        grid=(bsz, nsteps),
        in_specs=[blk(_CONV_DIM), blk(_DT_W), const((1, _DT_W)), const((1, _DT_W)),
                  blk(_SSD_INNER), blk(_SSD_INNER), const((1, _SSD_INNER)), const(epair.shape)],
        out_specs=blk(_SSD_INNER),
        out_shape=jax.ShapeDtypeStruct((bsz, seq, _SSD_INNER), _BF16),
        scratch_shapes=_ssd_scratch() + [pltpu.VMEM((_CHUNK, _SSD_INNER), _F32)],
        compiler_params=_cparams("parallel", "arbitrary"),
        name="ssd_bwd",
    )(act, dt3, dtb, alog, yf, z3, norm_g, epair)


_QB = 128


_QSUB = 4


def _attn_body(q_ref, kp_ref, kc_ref, kn_ref, vp_ref, vc_ref, vn_ref, o_ref, lse_ref, *, nstep):
    n = pl.program_id(2)
    ii = lax.broadcasted_iota(jnp.int32, (_QB, _QB), 0)
    jj = lax.broadcasted_iota(jnp.int32, (_QB, _QB), 1)
    dlt = jj - ii
    band = (dlt >= _QB - _ATT_RADIUS, jnp.abs(dlt) <= _ATT_RADIUS, dlt <= _ATT_RADIUS - _QB)
    lane_lo = lax.broadcasted_iota(jnp.int32, (1, _LANES), 1) < _ATT_HD
    nt = (((1,), (1,)), ((), ()))
    npair = _ATT_OUT // _LANES
    sl = [slice(p * _LANES, (p + 1) * _LANES) for p in range(npair)]

    def blocks(pr, cr, nr, p):
        return ([pr[0, 0, :, sl[p]]] + [cr[0, 0, s * _QB:(s + 1) * _QB, sl[p]] for s in range(_QSUB)]
                + [nr[0, 0, :, sl[p]]])

    ks = [blocks(kp_ref, kc_ref, kn_ref, p) for p in range(npair)]
    vs = [blocks(vp_ref, vc_ref, vn_ref, p) for p in range(npair)]
    chains = [(s, p, u) for s in range(_QSUB) for p in range(npair) for u in range(2)]
    scores = []
    for s, p, u in chains:
        qp = q_ref[0, 0, s * _QB:(s + 1) * _QB, sl[p]]
        hm = lane_lo if u == 0 else jnp.logical_not(lane_lo)
        qm = jnp.where(hm, qp, jnp.zeros_like(qp))
        masks = [band[0], band[1], band[2]]
        if s == 0:
            masks[0] = jnp.logical_and(band[0], n > 0)
        if s == _QSUB - 1:
            masks[2] = jnp.logical_and(band[2], n < nstep - 1)
        scores.append([jnp.where(m, lax.dot_general(qm, kx, nt, preferred_element_type=_F32), _NEG_BIG)
                       for m, kx in zip(masks, ks[p][s:s + 3])])
    mxs = [jnp.max(jnp.maximum(jnp.maximum(sc[0], sc[1]), sc[2]), axis=-1, keepdims=True) for sc in scores]
    probs = [[jnp.exp(sx - mx) for sx in sc] for sc, mx in zip(scores, mxs)]
    dens = [jnp.sum(pr[0] + pr[1] + pr[2], axis=-1, keepdims=True) for pr in probs]
    outs = []
    for (s, p, u), pr in zip(chains, probs):
        outs.append(sum(jnp.dot(px.astype(_BF16), vx, preferred_element_type=_F32)
                        for px, vx in zip(pr, vs[p][s:s + 3])))
    for c in range(0, len(chains), 2):
        s, p, _ = chains[c]
        rows = slice(s * _QB, (s + 1) * _QB)
        o0, o1 = outs[c] / dens[c], outs[c + 1] / dens[c + 1]
        l0 = jnp.broadcast_to(mxs[c] + jnp.log(dens[c]), (_QB, _LANES))
        l1 = jnp.broadcast_to(mxs[c + 1] + jnp.log(dens[c + 1]), (_QB, _LANES))
        o_ref[0, 0, rows, sl[p]] = jnp.where(lane_lo, o0, o1)
        lse_ref[0, 0, rows, sl[p]] = jnp.where(lane_lo, l0, l1)


def _attention_group(q, k, v, bsz, seq, dil):
    t = seq // dil
    nstep = t // (_QSUB * _QB)
    nblk = t // _QB
    own = pl.BlockSpec((1, 1, _QSUB * _QB, _ATT_OUT), lambda b, r, n: (b, r, n, 0))
    before = pl.BlockSpec((1, 1, _QB, _ATT_OUT), lambda b, r, n: (b, r, jnp.maximum(_QSUB * n - 1, 0), 0))
    after = pl.BlockSpec((1, 1, _QB, _ATT_OUT), lambda b, r, n: (b, r, jnp.minimum(_QSUB * (n + 1), nblk - 1), 0))
    return pl.pallas_call(
        functools.partial(_attn_body, nstep=nstep),
        grid=(bsz, dil, nstep),
        in_specs=[own, before, own, after, before, own, after],
        out_specs=[own, own],
        out_shape=[jax.ShapeDtypeStruct((bsz, dil, t, _ATT_OUT), _F32)] * 2,
        compiler_params=_cparams("parallel", "parallel", "parallel"),
        name=f"attn_d{dil}",
    )(q, k, k, k, v, v, v)


def _finish_body(yn_ref, o0_ref, l0_ref, o1_ref, l1_ref, o2_ref, l2_ref, gate_ref, x_ref,
                 wa_ref, wb_ref, wo_ref, ng_ref, wq_ref, x1_ref, h2_ref, qp_ref, mix_ref):
    tm = x_ref.shape[0]
    npair = _ATT_OUT // _LANES

    def token_order(ref, slot, dil):
        if dil == 1:
            return [ref[0, 0, :, p * _LANES:(p + 1) * _LANES] for p in range(npair)]
        for p in range(npair):
            for r in range(dil):
                mix_ref[slot * npair + p, pl.ds(r, tm // dil, stride=dil), :] = ref[0, r, :, p * _LANES:(p + 1) * _LANES]
        return [mix_ref[slot * npair + p] for p in range(npair)]

    dils = [d for _, d in _ATT_GROUPS]
    o_parts = [token_order(r, 2 * g, dils[g]) for g, r in enumerate((o0_ref, o1_ref, o2_ref))]
    l_parts = [token_order(r, 2 * g + 1, dils[g]) for g, r in enumerate((l0_ref, l1_ref, l2_ref))]
    halves = []
    for p in range(npair):
        l0, l1, l2 = l_parts[0][p], l_parts[1][p], l_parts[2][p]
        lm = jnp.maximum(jnp.maximum(l0, l1), l2)
        e0, e1, e2 = jnp.exp(l0 - lm), jnp.exp(l1 - lm), jnp.exp(l2 - lm)
        den = e0 + e1 + e2
        halves.append((e0 / den) * o_parts[0][p] + (e1 / den) * o_parts[1][p] + (e2 / den) * o_parts[2][p])
    o = jnp.concatenate(halves, axis=1)
    a = jnp.dot(yn_ref[...], wa_ref[...], preferred_element_type=_F32)
    b = jnp.dot(o.astype(_BF16), wb_ref[...], preferred_element_type=_F32)
    ga = _sigmoid(gate_ref[:, 0:_D].astype(_F32))
    gb = _sigmoid(gate_ref[:, _D:2 * _D].astype(_F32))
    merged = (ga * a + gb * b).astype(_BF16)
    x1 = x_ref[...] + jnp.dot(merged, wo_ref[...], preferred_element_type=_F32)
    x1_ref[...] = x1
    y = x1 * lax.rsqrt(jnp.mean(x1 * x1, axis=-1, keepdims=True) + _EPS)
    h2 = (y * ng_ref[...]).astype(_BF16)
    h2_ref[...] = h2
    for head in range(_PEER_HEADS):
        sl = slice(2 * head * _PEER_KEYS, 2 * (head + 1) * _PEER_KEYS)
        q2 = jnp.dot(h2, wq_ref[:, sl], preferred_element_type=_F32).astype(qp_ref.dtype)
        qp_ref[2 * head] = q2[:, 0:_PEER_KEYS]
        qp_ref[2 * head + 1] = q2[:, _PEER_KEYS:2 * _PEER_KEYS]


def _finish(yn, attn, gates, x2d, wa, wb, wo, ng, wq, seq, tm):
    n = x2d.shape[0]
    nseq = seq // tm
    row = lambda w: pl.BlockSpec((tm, w), lambda i: (i, 0))
    const = lambda shape: pl.BlockSpec(shape, lambda i: (0, 0), pipeline_mode=pl.Buffered(1))
    cls = lambda dil: pl.BlockSpec((1, dil, tm // dil, _ATT_OUT), lambda i: (i // nseq, 0, i % nseq, 0))
    attn_specs = [cls(dil) for _, dil in _ATT_GROUPS for _ in range(2)]
    nq = 2 * _PEER_HEADS
    return pl.pallas_call(
        _finish_body,
        grid=(n // tm,),
        in_specs=[row(_SSD_INNER)] + attn_specs + [row(_GATE_W), row(_D),
                  const((_SSD_INNER, _D)), const((_ATT_OUT, _D)), const((_D, _D)), const((1, _D)),
                  const((_D, nq * _PEER_KEYS))],
        out_specs=[row(_D), row(_D), pl.BlockSpec((nq, tm, _PEER_KEYS), lambda i: (0, i, 0))],
        out_shape=[jax.ShapeDtypeStruct((n, _D), _F32), jax.ShapeDtypeStruct((n, _D), _BF16),
                   jax.ShapeDtypeStruct((nq, n, _PEER_KEYS), _BF16)],
        scratch_shapes=[pltpu.VMEM((2 * len(_ATT_GROUPS) * (_ATT_OUT // _LANES), tm, _LANES), _F32)],
        compiler_params=_cparams("parallel"),
        name="finish",
    )(yn, *attn, gates, x2d, wa, wb, wo, ng, wq)


_ROUTE_TB = 1024
_ROUTE_SUB = _ROUTE_TB // _LANES
_ROUTE_CHAINS = 4


def _sweep_top(ref, nrows, k, codes_ref=None):
    sub = _ROUTE_SUB
    neg = jnp.full((sub, _LANES), -jnp.inf, _F32)
    vals, outs = [], []
    prev = None
    for r in range(k):
        best = [None] * _ROUTE_CHAINS
        for row in range(nrows):
            v = ref[row * sub:(row + 1) * sub, :]
            if prev is not None:
                v = jnp.where(prev == row, neg, v)
                ref[row * sub:(row + 1) * sub, :] = v
            c = row % _ROUTE_CHAINS
            rid = jnp.full((sub, _LANES), row, jnp.int32)
            code = codes_ref[row * sub:(row + 1) * sub, :] if codes_ref is not None else None
            if best[c] is None:
                best[c] = (v, rid, code)
            else:
                bv, bk, bc = best[c]
                better = v > bv
                best[c] = (jnp.maximum(bv, v), jnp.where(better, rid, bk),
                           jnp.where(better, code, bc) if code is not None else None)
        acc = best[0]
        for c in range(1, min(_ROUTE_CHAINS, nrows)):
            av, ak, ac = acc
            bv, bk, bc = best[c]
            take = jnp.logical_or(bv > av, jnp.logical_and(bv == av, bk < ak))
            acc = (jnp.maximum(av, bv), jnp.where(take, bk, ak), jnp.where(take, bc, ac) if ac is not None else None)
        vals.append(acc[0])
        outs.append(acc[2] if codes_ref is not None else acc[1])
        prev = acc[1]
    return vals, outs


def _route_body(qp_ref, keys_ref, i_ref, j_ref, g_ref, s_ref, cand_ref, code_ref, is_ref, js_ref, gs_ref):
    nt = (((1,), (1,)), ((), ()))
    k = _PEER_TOPK
    sub = _ROUTE_SUB
    pairs = [(a, b) for a in range(k) for b in range(k // (a + 1))]

    def head(h, carry):
        hv, hi = [], []
        for c in range(2):
            for a in range(sub):
                tile = lax.dot_general(keys_ref[2 * h + c], qp_ref[2 * h + c, a * _LANES:(a + 1) * _LANES, :], nt,
                                       preferred_element_type=_F32)
                s_ref[c, pl.ds(a, _PEER_KEYS, stride=sub), :] = tile
            v, i = _sweep_top(s_ref.at[c], _PEER_KEYS, k)
            hv.append(v)
            hi.append(i)
        for p, (a, b) in enumerate(pairs):
            cand_ref[p * sub:(p + 1) * sub, :] = hv[0][a] + hv[1][b]
            code_ref[p * sub:(p + 1) * sub, :] = hi[0][a] * _PEER_KEYS + hi[1][b]
        fs, codes = _sweep_top(cand_ref, len(pairs), k, codes_ref=code_ref)
        e = [jnp.exp(f - fs[0]) for f in fs]
        den = e[0]
        for x in e[1:]:
            den = den + x
        for r in range(k):
            r0 = pl.multiple_of((h * k + r) * sub, sub)
            is_ref[pl.ds(r0, sub), :] = jnp.right_shift(codes[r], 7).astype(_F32)
            js_ref[pl.ds(r0, sub), :] = jnp.bitwise_and(codes[r], _PEER_KEYS - 1).astype(_F32)
            gs_ref[pl.ds(r0, sub), :] = e[r] / den
        return carry

    lax.fori_loop(0, _PEER_HEADS, head, 0)
    slots = _PEER_HEADS * k
    for src, dst in ((is_ref, i_ref), (js_ref, j_ref), (gs_ref, g_ref)):
        for a in range(sub):
            dst[a * _LANES:(a + 1) * _LANES, :] = src[pl.ds(a, slots, stride=sub), :].T


def _peer_route(qp, sub_keys16):
    nq, n, _ = qp.shape
    tq = _ROUTE_TB
    slots = _PEER_HEADS * _PEER_TOPK
    ncand = sum(_PEER_TOPK // (a + 1) for a in range(_PEER_TOPK))
    out = pl.BlockSpec((tq, slots), lambda i: (i, 0))
    return pl.pallas_call(
        _route_body,
        grid=(n // tq,),
        in_specs=[pl.BlockSpec((nq, tq, _PEER_KEYS), lambda i: (0, i, 0)),
                  pl.BlockSpec((nq, _PEER_KEYS, _PEER_KEYS), lambda i: (0, 0, 0))],
        out_specs=[out] * 3,
        out_shape=[jax.ShapeDtypeStruct((n, slots), _F32)] * 3,
        scratch_shapes=[pltpu.VMEM((2, _PEER_KEYS * _ROUTE_SUB, _LANES), _F32),
                        pltpu.VMEM((ncand * _ROUTE_SUB, _LANES), _F32),
                        pltpu.VMEM((ncand * _ROUTE_SUB, _LANES), jnp.int32)]
        + [pltpu.VMEM((slots * _ROUTE_SUB, _LANES), _F32)] * 3,
        compiler_params=_cparams("parallel"),
        name="peer_route",
    )(qp, sub_keys16)


_PEER_TM = 512
_PEER_HALF = _PEER_TM // 2
_PEER_EB = 2048
_PEER_CW = 256
_PEER_PITCH = _PEER_HALF + 8
_PEER_UNROLL = 32


def _peer_body(h_ref, i_ref, j_ref, g_ref, x1_ref, u_ref, v_ref, o_ref, gate_ref, w_ref):
    e = pl.program_id(1)
    slots = _PEER_HEADS * _PEER_TOPK
    nt = (((1,), (1,)), ((), ()))
    hi_mask = jnp.uint32(0xFFFF0000)

    @pl.when(e == 0)
    def _():
        o_ref[...] = x1_ref[...]
        sub = lax.broadcasted_iota(jnp.int32, (_PEER_KEYS, slots), 0).astype(_F32).astype(_BF16)
        one = jnp.ones((), _BF16)
        zero = jnp.zeros((), _BF16)

        def gate_bits(n):
            irow = i_ref[pl.ds(n, 1), :].astype(_BF16)
            jrow = j_ref[pl.ds(n, 1), :].astype(_BF16)
            grow = g_ref[pl.ds(n, 1), :].astype(_BF16)
            p_t = jnp.where(sub == irow, one, zero)
            q_t = jnp.where(sub == jrow, grow, zero)
            gm = lax.dot_general(p_t, q_t, nt, preferred_element_type=_F32)
            return pltpu.bitcast(gm, jnp.uint32)

        def pair(n, carry):
            lo = jnp.right_shift(gate_bits(n), jnp.uint32(16))
            hi = jnp.bitwise_and(gate_bits(n + _PEER_HALF), hi_mask)
            gate_ref[pl.ds(n, _PEER_KEYS, stride=_PEER_PITCH), :] = jnp.bitwise_or(lo, hi)
            return carry

        lax.fori_loop(0, _PEER_HALF, pair, 0, unroll=_PEER_UNROLL)

    rows_per_chunk = _PEER_CW // _PEER_KEYS
    h = h_ref[...]
    for c in range(_PEER_EB // _PEER_CW):
        hmat = lax.dot_general(h, u_ref[c * _PEER_CW:(c + 1) * _PEER_CW, :], nt, preferred_element_type=_F32)
        act = 0.5 * hmat * (1.0 + lax.erf(hmat * (1.0 / math.sqrt(2.0))))
        cols = []
        for r in range(rows_per_chunk):
            row = e * (_PEER_EB // _PEER_KEYS) + c * rows_per_chunk + r
            word = gate_ref[pl.ds(pl.multiple_of(row * _PEER_PITCH, 8), _PEER_HALF), :]
            lo = pltpu.bitcast(jnp.left_shift(word, jnp.uint32(16)), _F32)
            hi = pltpu.bitcast(jnp.bitwise_and(word, hi_mask), _F32)
            cols.append(jnp.concatenate([lo, hi], axis=0))
        gate = jnp.concatenate(cols, axis=1)
        w_ref[:, c * _PEER_CW:(c + 1) * _PEER_CW] = (gate * act).astype(_BF16)
    o_ref[...] += jnp.dot(w_ref[...], v_ref[...], preferred_element_type=_F32)


def _peer_dense(h2, ri, rj, rg, x1, u16, v16):
    n = h2.shape[0]
    tm, eb = _PEER_TM, _PEER_EB
    slots = _PEER_HEADS * _PEER_TOPK
    nexp = u16.shape[0]
    tok = lambda w: pl.BlockSpec((tm, w), lambda i, e: (i, 0))
    return pl.pallas_call(
        _peer_body,
        grid=(n // tm, nexp // eb),
        in_specs=[tok(_D), tok(slots), tok(slots), tok(slots), tok(_D),
                  pl.BlockSpec((eb, _D), lambda i, e: (e, 0)),
                  pl.BlockSpec((eb, _D), lambda i, e: (e, 0))],
        out_specs=tok(_D),
        out_shape=jax.ShapeDtypeStruct((n, _D), _F32),
        scratch_shapes=[pltpu.VMEM((_PEER_KEYS * _PEER_PITCH, _PEER_KEYS), jnp.uint32),
                        pltpu.VMEM((tm, eb), _BF16)],
        compiler_params=_cparams("parallel", "arbitrary"),
        name="peer_dense",
    )(h2, ri, rj, rg, x1, u16, v16)


def _prepare(norm_mix_g, w_in, conv_w, conv_b, dt_bias, a_log, d_skip, ssd_norm_g, w_a,
             q_norm_g, k_norm_g, w_b, w_o, norm_ffn_g, w_query, sub_keys, expert_u, expert_v):
    xbc_end = _SSD_INNER + _CONV_DIM
    dt_end = xbc_end + 2 * _SSD_HEADS
    pad_dt = lambda a: jnp.pad(a.astype(_F32).reshape(1, 2 * _SSD_HEADS), ((0, 0), (0, _DT_W - 2 * _SSD_HEADS)))
    row = lambda a: a.astype(_F32)[None, :]
    return dict(
        w_perm=jnp.concatenate(
            [w_in[:, :xbc_end], w_in[:, dt_end:], w_in[:, xbc_end:dt_end],
             jnp.zeros((_D, _DT_W - 2 * _SSD_HEADS), w_in.dtype)], axis=1).astype(_BF16),
        norm_mix_g=row(norm_mix_g), conv_w=conv_w.astype(_F32), conv_b=row(conv_b),
        dt_bias=pad_dt(dt_bias), a_log=pad_dt(a_log), d_skip=jnp.repeat(d_skip.astype(_F32), _SSD_HEAD_DIM)[None, :],
        ssd_norm_g=row(ssd_norm_g), q_norm_g=q_norm_g, k_norm_g=k_norm_g,
        w_a=w_a.astype(_BF16), w_b=w_b.astype(_BF16), w_o=w_o.astype(_BF16), norm_ffn_g=row(norm_ffn_g),
        w_query=w_query.astype(_BF16),
        keys16=sub_keys.reshape(2 * _PEER_HEADS, _PEER_KEYS, sub_keys.shape[-1]).astype(_BF16),
        u16=expert_u.astype(_BF16), v16=expert_v.astype(_BF16),
    )


def _layer(x, w):
    bsz, seq, d_model = x.shape
    n = bsz * seq
    max_dil = max(d for _, d in _ATT_GROUPS)
    assert d_model == _D and seq % _TOKEN_TILE == 0 and seq % _SSD_STEP == 0, x.shape
    assert seq % (max_dil * _QSUB * _QB) == 0 and _TOKEN_TILE % (max_dil * _HALO) == 0, x.shape
    assert n % _ROUTE_TB == 0 and n % _PEER_TM == 0, x.shape
    x2d = x.reshape(n, _D)
    z, xbc, qkv, gates, dt = _in_proj(x2d, w["norm_mix_g"], w["w_perm"], tm=_TOKEN_TILE)
    qkv_groups = _qk_prep(qkv, w["q_norm_g"], w["k_norm_g"], bsz, seq, tm=_TOKEN_TILE)
    act, yf = _ssd_fwd(xbc, dt, w["conv_w"], w["conv_b"], w["dt_bias"], w["a_log"], w["d_skip"], bsz, seq)
    yn = _ssd_bwd(act, dt, w["dt_bias"], w["a_log"], yf, z, w["ssd_norm_g"], bsz, seq)
    attn = []
    for gi, (window, dil) in enumerate(_ATT_GROUPS):
        assert window // (2 * dil) == _ATT_RADIUS
        attn.extend(_attention_group(*qkv_groups[3 * gi:3 * gi + 3], bsz, seq, dil))
    x1, h2, qp = _finish(yn.reshape(n, _SSD_INNER), attn, gates, x2d, w["w_a"], w["w_b"], w["w_o"],
                         w["norm_ffn_g"], w["w_query"], seq, tm=_TOKEN_TILE)
    ri, rj, rg = _peer_route(qp, w["keys16"])
    out = _peer_dense(h2, ri, rj, rg, x1, w["u16"], w["v16"])
    return out.reshape(bsz, seq, _D)


def kernel(x_prompt, x_sample, norm_mix_g, w_in, conv_w, conv_b, dt_bias, a_log, d_skip, ssd_norm_g, w_a,
           q_norm_g, k_norm_g, w_b, w_o, norm_ffn_g, w_query, sub_keys, expert_u, expert_v):
    weights = (norm_mix_g, w_in, conv_w, conv_b, dt_bias, a_log, d_skip, ssd_norm_g, w_a,
               q_norm_g, k_norm_g, w_b, w_o, norm_ffn_g, w_query, sub_keys, expert_u, expert_v)
    outs = [x_prompt, x_sample]
    for l in range(w_in.shape[0]):
        w = _prepare(*(a[l] for a in weights))
        outs = [_layer(x, w) for x in outs]
    return tuple(outs)
```

```python
import functools
import math

import jax
import jax.numpy as jnp
from jax import lax
from jax.experimental import pallas as pl
from jax.experimental.pallas import tpu as pltpu

_F32 = jnp.float32
_BF16 = jnp.bfloat16
_HIGHEST = lax.Precision.HIGHEST

_V7X_VMEM_BYTES = 64 * 1024 * 1024
_VMEM_LIMIT = _V7X_VMEM_BYTES - 8 * 1024 * 1024
_LANES = 128
_TOKEN_TILE = 512

_EPS = 1e-6
_D = 1024
_SSD_HEADS = 32
_SSD_HEAD_DIM = 64
_SSD_INNER = 2048
_SSD_GROUPS = 4
_SSD_STATE = 128
_SSD_BC = 512
_CONV_W = 5
_CONV_DIM = 3072
_CHUNK = 128
_HALO = 16
_ATT_GROUPS = ((128, 1), (512, 4), (2048, 16))
_ATT_WIDTH = 768
_ATT_OUT = 256
_ATT_HD = 64
_ATT_RADIUS = 64
_ROPE_DIMS = 16
_ROPE_THETA = 500000.0
_NEG_BIG = -1e30
_PEER_HEADS = 8
_PEER_KEYS = 128
_PEER_TOPK = 16
_QKV_W = 3 * _ATT_WIDTH
_GATE_W = 2 * _D
_DT_W = 128
_IN_COLS_PAD = _SSD_INNER + _CONV_DIM + _QKV_W + _GATE_W + _DT_W


def _cparams(*sem):
    return pltpu.CompilerParams(dimension_semantics=sem, vmem_limit_bytes=_VMEM_LIMIT)


def _sigmoid(x):
    return 1.0 / (1.0 + jnp.exp(-x))


_COL_CHUNK = 512


def _inproj_body(x_ref, g_ref, w_ref, z_ref, xbc_ref, qkv_ref, gate_ref, dt_ref):
    x = x_ref[...]
    y = x * lax.rsqrt(jnp.mean(x * x, axis=-1, keepdims=True) + _EPS)
    h = (y * g_ref[...]).astype(_BF16)
    col = 0
    for ref in (z_ref, xbc_ref, qkv_ref, gate_ref, dt_ref):
        width = ref.shape[-1]
        for c0 in range(0, width, _COL_CHUNK):
            cw = min(_COL_CHUNK, width - c0)
            acc = jnp.dot(h, w_ref[:, col + c0:col + c0 + cw], preferred_element_type=_F32)
            ref[:, c0:c0 + cw] = acc.astype(ref.dtype)
        col += width


def _in_proj(x2d, g, w_perm, tm):
    n = x2d.shape[0]
    widths = (_SSD_INNER, _CONV_DIM, _QKV_W, _GATE_W, _DT_W)
    dtypes = (_BF16, _BF16, _BF16, _BF16, _F32)
    return pl.pallas_call(
        _inproj_body,
        grid=(n // tm,),
        in_specs=[
            pl.BlockSpec((tm, _D), lambda i: (i, 0)),
            pl.BlockSpec((1, _D), lambda i: (0, 0)),
            pl.BlockSpec((_D, _IN_COLS_PAD), lambda i: (0, 0), pipeline_mode=pl.Buffered(1)),
        ],
        out_specs=[pl.BlockSpec((tm, w), lambda i: (i, 0)) for w in widths],
        out_shape=[jax.ShapeDtypeStruct((n, w), dt) for w, dt in zip(widths, dtypes)],
        compiler_params=_cparams("parallel"),
        name="in_proj",
    )(x2d, g, w_perm)


def _qkprep_body(q_ref, k_ref, v_ref, qg_ref, kg_ref, c_ref, s1_ref, s2_ref, bd_ref, *refs):
    outs, tmp_ref = refs[:-1], refs[-1]
    tm = q_ref.shape[0]
    cosv = c_ref[...]
    s1 = s1_ref[...]
    s2 = s2_ref[...]
    bd = bd_ref[...]
    cols_per_group = _ATT_OUT // _LANES
    for ti, (src, g_ref, scale) in enumerate(((q_ref, qg_ref, _ATT_HD ** -0.5), (k_ref, kg_ref, 1.0), (v_ref, None, 1.0))):
        for p in range(_ATT_WIDTH // _LANES):
            x = src[:, p * _LANES:(p + 1) * _LANES].astype(_F32)
            if g_ref is not None:
                ss = jnp.dot(_split3(x * x), bd, preferred_element_type=_F32)
                y = x * lax.rsqrt(ss * (1.0 / _ATT_HD) + _EPS) * g_ref[...]
                x = y * cosv + pltpu.roll(y, 8, 1) * s1 + pltpu.roll(y, _LANES - 8, 1) * s2
                if scale != 1.0:
                    x = x * scale
            gi, pl_ = divmod(p, cols_per_group)
            dil = _ATT_GROUPS[gi][1]
            dst = outs[3 * gi + ti]
            lanes = slice(pl_ * _LANES, (pl_ + 1) * _LANES)
            if dil == 1:
                dst[0, 0, :, lanes] = x.astype(dst.dtype)
            else:
                tmp_ref[...] = x
                for r in range(dil):
                    dst[0, r, :, lanes] = tmp_ref[pl.ds(r, tm // dil, stride=dil), :].astype(dst.dtype)


def _rope_tables(seq):
    half = _ROPE_DIMS // 2
    inv = _ROPE_THETA ** (-jnp.arange(half, dtype=_F32) * 2.0 / _ROPE_DIMS)
    ang = jnp.arange(seq, dtype=_F32)[:, None] * inv[None, :]
    cos, sin = jnp.cos(ang), jnp.sin(ang)
    ones = jnp.ones((seq, _ATT_HD - _ROPE_DIMS), _F32)
    zeros = jnp.zeros((seq, _ATT_HD - _ROPE_DIMS), _F32)
    zh = jnp.zeros((seq, half), _F32)
    c = jnp.concatenate([cos, cos, ones], axis=1)
    s1 = jnp.concatenate([zh, sin, zeros], axis=1)
    s2 = jnp.concatenate([-sin, zh, zeros], axis=1)
    rep = _LANES // _ATT_HD
    return tuple(jnp.tile(t, (1, rep)) for t in (c, s1, s2))


def _qk_prep(qkv, q_g, k_g, bsz, seq, tm):
    c, s1, s2 = _rope_tables(seq)
    nseq = seq // tm
    lane = jnp.arange(_LANES)
    bd = jnp.tile((lane[:, None] // _ATT_HD == lane[None, :] // _ATT_HD).astype(_BF16), (3, 1))
    rep = _LANES // _ATT_HD
    qg = jnp.tile(q_g.astype(_F32), rep)[None, :]
    kg = jnp.tile(k_g.astype(_F32), rep)[None, :]
    tab = pl.BlockSpec((tm, _LANES), lambda b, i: (i, 0))
    const = lambda shape: pl.BlockSpec(shape, lambda b, i: (0, 0))
    col = lambda cblk: pl.BlockSpec((tm, _ATT_WIDTH), lambda b, i: (b * nseq + i, cblk))
    out_specs, out_shapes = [], []
    for _, dil in _ATT_GROUPS:
        for _ in range(3):
            out_specs.append(pl.BlockSpec((1, dil, tm // dil, _ATT_OUT), lambda b, i: (b, 0, i, 0)))
            out_shapes.append(jax.ShapeDtypeStruct((bsz, dil, seq // dil, _ATT_OUT), _BF16))
    return pl.pallas_call(
        _qkprep_body,
        grid=(bsz, nseq),
        in_specs=[col(0), col(1), col(2), const((1, _LANES)), const((1, _LANES)), tab, tab, tab,
                  const((3 * _LANES, _LANES))],
        out_specs=out_specs,
        out_shape=out_shapes,
        scratch_shapes=[pltpu.VMEM((tm, _LANES), _F32)],
        compiler_params=_cparams("parallel", "parallel"),
        name="qk_prep",
    )(qkv, qkv, qkv, qg, kg, c, s1, s2, bd)


def _pair_bcast(a, lane_lo, h):
    rows = a.shape[0]
    lo = jnp.broadcast_to(a[:, h:h + 1], (rows, _LANES))
    hi = jnp.broadcast_to(a[:, h + 1:h + 2], (rows, _LANES))
    return jnp.where(lane_lo, lo, hi)


def _split3(x):
    hi = x.astype(_BF16)
    r1 = x - hi.astype(_F32)
    mid = r1.astype(_BF16)
    lo = (r1 - mid.astype(_F32)).astype(_BF16)
    return jnp.concatenate([hi, mid, lo], axis=1)


def _ssd_chunk(act_ref, dt, da, rev, epair_ref, state_ref, xsc_ref, emit):
    L = _CHUNK
    row = lax.broadcasted_iota(jnp.int32, (L, L), 0)
    col = lax.broadcasted_iota(jnp.int32, (L, L), 1)
    keep = (row <= col) if rev else (row >= col)
    keep_t = (row >= col) if rev else (row <= col)
    cs = jnp.dot(keep.astype(_F32), da, precision=_HIGHEST, preferred_element_type=_F32)
    cs_t = jnp.dot(da.T, keep_t.astype(_F32), precision=_HIGHEST, preferred_element_type=_F32)
    dt_t = dt.T
    cs_end = cs[0:1, :] if rev else cs[L - 1:L, :]
    cdec = jnp.exp(cs_end)
    ecs = jnp.exp(cs)
    wst = jnp.exp(cs_end - cs) * dt
    lane_lo = lax.broadcasted_iota(jnp.int32, (1, _LANES), 1) < _SSD_HEAD_DIM
    if epair_ref is not None:
        ecs_all = jnp.dot(_split3(ecs), epair_ref[...], preferred_element_type=_F32)
        wst_all = jnp.dot(_split3(wst), epair_ref[...], preferred_element_type=_F32)
        xsc_ref[...] = (act_ref[:, 0:_SSD_INNER] * wst_all).astype(_BF16)
    heads_per_group = _SSD_HEADS // _SSD_GROUPS
    pairs_per_group = heads_per_group // 2
    gw = heads_per_group * _SSD_HEAD_DIM
    for g in range(_SSD_GROUPS):
        bm = act_ref[:, _SSD_INNER + g * _SSD_STATE:_SSD_INNER + (g + 1) * _SSD_STATE]
        cm = act_ref[:, _SSD_INNER + _SSD_BC + g * _SSD_STATE:_SSD_INNER + _SSD_BC + (g + 1) * _SSD_STATE]
        bm16 = bm.astype(_BF16)
        cm16 = cm.astype(_BF16)
        cb = lax.dot_general(cm16, bm16, (((1,), (1,)), ((), ())), preferred_element_type=_F32)
        y_off = jnp.dot(cm16, state_ref[g].astype(_BF16), preferred_element_type=_F32)
        cdec_cols = []
        for jj in range(pairs_per_group):
            j = g * pairs_per_group + jj
            sl = slice(j * _LANES, (j + 1) * _LANES)
            xs = act_ref[:, sl]
            xs16 = xs.astype(_BF16)
            res = []
            for u in range(2):
                h = 2 * j + u
                diff = jnp.broadcast_to(cs[:, h:h + 1], (L, L)) - cs_t[h:h + 1, :]
                decay = jnp.exp(jnp.where(keep, diff, _NEG_BIG))
                m = (cb * decay * dt_t[h:h + 1, :]).astype(_BF16)
                res.append(jnp.dot(m, xs16, preferred_element_type=_F32))
            y = jnp.where(lane_lo, res[0], res[1])
            if epair_ref is not None:
                ec = ecs_all[:, sl]
            else:
                ec = _pair_bcast(ecs, lane_lo, 2 * j)
                xsc_ref[:, sl] = (xs * _pair_bcast(wst, lane_lo, 2 * j)).astype(_BF16)
            emit(j, y + y_off[:, jj * _LANES:(jj + 1) * _LANES] * ec)
            cdec_cols.append(_pair_bcast(cdec, lane_lo, 2 * j))
        new = jnp.dot(bm.T.astype(_BF16), xsc_ref[:, g * gw:(g + 1) * gw], preferred_element_type=_F32)
        state_ref[g] = state_ref[g] * jnp.concatenate(cdec_cols, axis=1) + new


def _softplus(x):
    return jnp.maximum(x, 0.0) + jnp.log(1.0 + jnp.exp(-jnp.abs(x)))


_SSD_STEP_CHUNKS = 4
_SSD_STEP = _SSD_STEP_CHUNKS * _CHUNK


def _ssd_fwd_body(xp_ref, xc_ref, xn_ref, dt_ref, cw_ref, cb_ref, dtb_ref, alog_ref, dskip_ref,
                  act_out_ref, y_ref, win_ref, act_ref, state_ref, xsc_ref, *, nsteps):
    c = pl.program_id(1)

    @pl.when(c == 0)
    def _():
        state_ref[...] = jnp.zeros_like(state_ref)

    L = _CHUNK
    prev_ok = (c > 0).astype(_F32)
    next_ok = (c < nsteps - 1).astype(_F32)
    win_ref[0:_HALO, :] = xp_ref[0].astype(_F32) * prev_ok
    win_ref[_HALO:_HALO + _SSD_STEP, :] = xc_ref[0].astype(_F32)
    win_ref[_HALO + _SSD_STEP:2 * _HALO + _SSD_STEP, :] = xn_ref[0].astype(_F32) * next_ok
    pad = (_CONV_W - 1) // 2
    cw = cw_ref[...]
    a_neg = -jnp.exp(alog_ref[...])
    for s in range(_SSD_STEP_CHUNKS):
        rows = slice(s * L, (s + 1) * L)
        for c0 in range(0, _CONV_DIM, _COL_CHUNK):
            sl = slice(c0, c0 + _COL_CHUNK)
            acc = jnp.broadcast_to(cb_ref[:, sl], (L, _COL_CHUNK))
            for k in range(_CONV_W):
                r0 = _HALO + s * L - pad + k
                acc = acc + win_ref[r0:r0 + L, sl] * cw[k:k + 1, sl]
            a = acc * _sigmoid(acc)
            act_ref[:, sl] = a
            act_out_ref[0, rows, sl] = a.astype(act_out_ref.dtype)
        dt = _softplus(dt_ref[0, rows, :] + dtb_ref[...])

        def emit(j, y, rows=rows):
            sl = slice(j * _LANES, (j + 1) * _LANES)
            y_ref[0, rows, sl] = y + dskip_ref[:, sl] * act_ref[:, sl]

        _ssd_chunk(act_ref, dt, dt * a_neg, False, None, state_ref, xsc_ref, emit)


def _ssd_bwd_body(act_in_ref, dt_ref, dtb_ref, alog_ref, yf_ref, z_ref, ng_ref, epair_ref,
                  yn_ref, act_ref, state_ref, xsc_ref, ytot_ref):
    c = pl.program_id(1)

    @pl.when(c == 0)
    def _():
        state_ref[...] = jnp.zeros_like(state_ref)

    L = _CHUNK
    shift = _LANES - _SSD_HEADS
    dtb = pltpu.roll(dtb_ref[...], shift, 1)
    a_neg = -jnp.exp(pltpu.roll(alog_ref[...], shift, 1))
    gw = _SSD_INNER // _SSD_GROUPS
    for s in reversed(range(_SSD_STEP_CHUNKS)):
        rows = slice(s * L, (s + 1) * L)
        act_ref[...] = act_in_ref[0, rows, :].astype(_F32)
        dt = _softplus(pltpu.roll(dt_ref[0, rows, :], shift, 1) + dtb)

        def emit(j, y, rows=rows):
            sl = slice(j * _LANES, (j + 1) * _LANES)
            z = z_ref[0, rows, sl].astype(_F32)
            ytot_ref[:, sl] = (y + yf_ref[0, rows, sl]) * (z * _sigmoid(z))

        _ssd_chunk(act_ref, dt, dt * a_neg, True, epair_ref, state_ref, xsc_ref, emit)
        for g in range(_SSD_GROUPS):
            t = ytot_ref[:, g * gw:(g + 1) * gw]
            r = lax.rsqrt(jnp.mean(t * t, axis=-1, keepdims=True) + _EPS)
            yn_ref[0, rows, g * gw:(g + 1) * gw] = (t * r * ng_ref[:, g * gw:(g + 1) * gw]).astype(yn_ref.dtype)


def _spread_matrix():
    r = jnp.arange(3 * _LANES)[:, None] % _LANES
    return (r == jnp.arange(_SSD_INNER)[None, :] // _SSD_HEAD_DIM).astype(_BF16)


def _ssd_scratch():
    return [
        pltpu.VMEM((_CHUNK, _CONV_DIM), _F32),
        pltpu.VMEM((_SSD_GROUPS, _SSD_STATE, _SSD_INNER // _SSD_GROUPS), _F32),
        pltpu.VMEM((_CHUNK, _SSD_INNER), _BF16),
    ]


def _ssd_fwd(xbc, dt, conv_w, conv_b, dtb, alog, dskip, bsz, seq):
    nsteps = seq // _SSD_STEP
    hb = _SSD_STEP // _HALO
    xbc3 = xbc.reshape(bsz, seq, _CONV_DIM)
    dt3 = dt.reshape(bsz, seq, _DT_W)
    const = lambda shape: pl.BlockSpec(shape, lambda b, c: (0,) * len(shape))
    act, yf = pl.pallas_call(
        functools.partial(_ssd_fwd_body, nsteps=nsteps),
        grid=(bsz, nsteps),
        in_specs=[
            pl.BlockSpec((1, _HALO, _CONV_DIM), lambda b, c: (b, jnp.maximum(c * hb - 1, 0), 0)),
            pl.BlockSpec((1, _SSD_STEP, _CONV_DIM), lambda b, c: (b, c, 0)),
            pl.BlockSpec((1, _HALO, _CONV_DIM), lambda b, c: (b, jnp.minimum((c + 1) * hb, nsteps * hb - 1), 0)),
            pl.BlockSpec((1, _SSD_STEP, _DT_W), lambda b, c: (b, c, 0)),
            const((_CONV_W, _CONV_DIM)), const((1, _CONV_DIM)), const((1, _DT_W)), const((1, _DT_W)),
            const((1, _SSD_INNER)),
        ],
        out_specs=[
            pl.BlockSpec((1, _SSD_STEP, _CONV_DIM), lambda b, c: (b, c, 0)),
            pl.BlockSpec((1, _SSD_STEP, _SSD_INNER), lambda b, c: (b, c, 0)),
        ],
        out_shape=[
            jax.ShapeDtypeStruct((bsz, seq, _CONV_DIM), _BF16),
            jax.ShapeDtypeStruct((bsz, seq, _SSD_INNER), _F32),
        ],
        scratch_shapes=[pltpu.VMEM((_SSD_STEP + 2 * _HALO, _CONV_DIM), _F32)] + _ssd_scratch(),
        compiler_params=_cparams("parallel", "arbitrary"),
        name="ssd_fwd",
    )(xbc3, xbc3, xbc3, dt3, conv_w, conv_b, dtb, alog, dskip)
    return act, yf


def _ssd_bwd(act, dt, dtb, alog, yf, z, norm_g, bsz, seq):
    nsteps = seq // _SSD_STEP
    dt3 = dt.reshape(bsz, seq, _DT_W)
    z3 = z.reshape(bsz, seq, _SSD_INNER)
    epair = _spread_matrix()
    const = lambda shape: pl.BlockSpec(shape, lambda b, c: (0,) * len(shape))
    blk = lambda w: pl.BlockSpec((1, _SSD_STEP, w), lambda b, c: (b, nsteps - 1 - c, 0))
    return pl.pallas_call(
        _ssd_bwd_body,
        grid=(bsz, nsteps),
        in_specs=[blk(_CONV_DIM), blk(_DT_W), const((1, _DT_W)), const((1, _DT_W)),
                  blk(_SSD_INNER), blk(_SSD_INNER), const((1, _SSD_INNER)), const(epair.shape)],
        out_specs=blk(_SSD_INNER),
        out_shape=jax.ShapeDtypeStruct((bsz, seq, _SSD_INNER), _BF16),
        scratch_shapes=_ssd_scratch() + [pltpu.VMEM((_CHUNK, _SSD_INNER), _F32)],
        compiler_params=_cparams("parallel", "arbitrary"),
        name="ssd_bwd",
    )(act, dt3, dtb, alog, yf, z3, norm_g, epair)


_QB = 128


_QSUB = 4


def _attn_body(q_ref, kp_ref, kc_ref, kn_ref, vp_ref, vc_ref, vn_ref, o_ref, lse_ref, *, nstep):
    n = pl.program_id(2)
    ii = lax.broadcasted_iota(jnp.int32, (_QB, _QB), 0)
    jj = lax.broadcasted_iota(jnp.int32, (_QB, _QB), 1)
    dlt = jj - ii
    band = (dlt >= _QB - _ATT_RADIUS, jnp.abs(dlt) <= _ATT_RADIUS, dlt <= _ATT_RADIUS - _QB)
    lane_lo = lax.broadcasted_iota(jnp.int32, (1, _LANES), 1) < _ATT_HD
    nt = (((1,), (1,)), ((), ()))
    npair = _ATT_OUT // _LANES
    sl = [slice(p * _LANES, (p + 1) * _LANES) for p in range(npair)]

    def blocks(pr, cr, nr, p):
        return ([pr[0, 0, :, sl[p]]] + [cr[0, 0, s * _QB:(s + 1) * _QB, sl[p]] for s in range(_QSUB)]
                + [nr[0, 0, :, sl[p]]])

    ks = [blocks(kp_ref, kc_ref, kn_ref, p) for p in range(npair)]
    vs = [blocks(vp_ref, vc_ref, vn_ref, p) for p in range(npair)]
    chains = [(s, p, u) for s in range(_QSUB) for p in range(npair) for u in range(2)]
    scores = []
    for s, p, u in chains:
        qp = q_ref[0, 0, s * _QB:(s + 1) * _QB, sl[p]]
        hm = lane_lo if u == 0 else jnp.logical_not(lane_lo)
        qm = jnp.where(hm, qp, jnp.zeros_like(qp))
        masks = [band[0], band[1], band[2]]
        if s == 0:
            masks[0] = jnp.logical_and(band[0], n > 0)
        if s == _QSUB - 1:
            masks[2] = jnp.logical_and(band[2], n < nstep - 1)
        scores.append([jnp.where(m, lax.dot_general(qm, kx, nt, preferred_element_type=_F32), _NEG_BIG)
                       for m, kx in zip(masks, ks[p][s:s + 3])])
    mxs = [jnp.max(jnp.maximum(jnp.maximum(sc[0], sc[1]), sc[2]), axis=-1, keepdims=True) for sc in scores]
    probs = [[jnp.exp(sx - mx) for sx in sc] for sc, mx in zip(scores, mxs)]
    dens = [jnp.sum(pr[0] + pr[1] + pr[2], axis=-1, keepdims=True) for pr in probs]
    outs = []
    for (s, p, u), pr in zip(chains, probs):
        outs.append(sum(jnp.dot(px.astype(_BF16), vx, preferred_element_type=_F32)
                        for px, vx in zip(pr, vs[p][s:s + 3])))
    for c in range(0, len(chains), 2):
        s, p, _ = chains[c]
        rows = slice(s * _QB, (s + 1) * _QB)
        o0, o1 = outs[c] / dens[c], outs[c + 1] / dens[c + 1]
        l0 = jnp.broadcast_to(mxs[c] + jnp.log(dens[c]), (_QB, _LANES))
        l1 = jnp.broadcast_to(mxs[c + 1] + jnp.log(dens[c + 1]), (_QB, _LANES))
        o_ref[0, 0, rows, sl[p]] = jnp.where(lane_lo, o0, o1)
        lse_ref[0, 0, rows, sl[p]] = jnp.where(lane_lo, l0, l1)


def _attention_group(q, k, v, bsz, seq, dil):
    t = seq // dil
    nstep = t // (_QSUB * _QB)
    nblk = t // _QB
    own = pl.BlockSpec((1, 1, _QSUB * _QB, _ATT_OUT), lambda b, r, n: (b, r, n, 0))
    before = pl.BlockSpec((1, 1, _QB, _ATT_OUT), lambda b, r, n: (b, r, jnp.maximum(_QSUB * n - 1, 0), 0))
    after = pl.BlockSpec((1, 1, _QB, _ATT_OUT), lambda b, r, n: (b, r, jnp.minimum(_QSUB * (n + 1), nblk - 1), 0))
    return pl.pallas_call(
        functools.partial(_attn_body, nstep=nstep),
        grid=(bsz, dil, nstep),
        in_specs=[own, before, own, after, before, own, after],
        out_specs=[own, own],
        out_shape=[jax.ShapeDtypeStruct((bsz, dil, t, _ATT_OUT), _F32)] * 2,
        compiler_params=_cparams("parallel", "parallel", "parallel"),
        name=f"attn_d{dil}",
    )(q, k, k, k, v, v, v)


def _finish_body(yn_ref, o0_ref, l0_ref, o1_ref, l1_ref, o2_ref, l2_ref, gate_ref, x_ref,
                 wa_ref, wb_ref, wo_ref, ng_ref, wq_ref, x1_ref, h2_ref, qp_ref, mix_ref):
    tm = x_ref.shape[0]
    npair = _ATT_OUT // _LANES

    def token_order(ref, slot, dil):
        if dil == 1:
            return [ref[0, 0, :, p * _LANES:(p + 1) * _LANES] for p in range(npair)]
        for p in range(npair):
            for r in range(dil):
                mix_ref[slot * npair + p, pl.ds(r, tm // dil, stride=dil), :] = ref[0, r, :, p * _LANES:(p + 1) * _LANES]
        return [mix_ref[slot * npair + p] for p in range(npair)]

    dils = [d for _, d in _ATT_GROUPS]
    o_parts = [token_order(r, 2 * g, dils[g]) for g, r in enumerate((o0_ref, o1_ref, o2_ref))]
    l_parts = [token_order(r, 2 * g + 1, dils[g]) for g, r in enumerate((l0_ref, l1_ref, l2_ref))]
    halves = []
    for p in range(npair):
        l0, l1, l2 = l_parts[0][p], l_parts[1][p], l_parts[2][p]
        lm = jnp.maximum(jnp.maximum(l0, l1), l2)
        e0, e1, e2 = jnp.exp(l0 - lm), jnp.exp(l1 - lm), jnp.exp(l2 - lm)
        den = e0 + e1 + e2
        halves.append((e0 / den) * o_parts[0][p] + (e1 / den) * o_parts[1][p] + (e2 / den) * o_parts[2][p])
    o = jnp.concatenate(halves, axis=1)
    a = jnp.dot(yn_ref[...], wa_ref[...], preferred_element_type=_F32)
    b = jnp.dot(o.astype(_BF16), wb_ref[...], preferred_element_type=_F32)
    ga = _sigmoid(gate_ref[:, 0:_D].astype(_F32))
    gb = _sigmoid(gate_ref[:, _D:2 * _D].astype(_F32))
    merged = (ga * a + gb * b).astype(_BF16)
    x1 = x_ref[...] + jnp.dot(merged, wo_ref[...], preferred_element_type=_F32)
    x1_ref[...] = x1
    y = x1 * lax.rsqrt(jnp.mean(x1 * x1, axis=-1, keepdims=True) + _EPS)
    h2 = (y * ng_ref[...]).astype(_BF16)
    h2_ref[...] = h2
    for head in range(_PEER_HEADS):
        sl = slice(2 * head * _PEER_KEYS, 2 * (head + 1) * _PEER_KEYS)
        q2 = jnp.dot(h2, wq_ref[:, sl], preferred_element_type=_F32).astype(qp_ref.dtype)
        qp_ref[2 * head] = q2[:, 0:_PEER_KEYS]
        qp_ref[2 * head + 1] = q2[:, _PEER_KEYS:2 * _PEER_KEYS]


def _finish(yn, attn, gates, x2d, wa, wb, wo, ng, wq, seq, tm):
    n = x2d.shape[0]
    nseq = seq // tm
    row = lambda w: pl.BlockSpec((tm, w), lambda i: (i, 0))
    const = lambda shape: pl.BlockSpec(shape, lambda i: (0, 0), pipeline_mode=pl.Buffered(1))
    cls = lambda dil: pl.BlockSpec((1, dil, tm // dil, _ATT_OUT), lambda i: (i // nseq, 0, i % nseq, 0))
    attn_specs = [cls(dil) for _, dil in _ATT_GROUPS for _ in range(2)]
    nq = 2 * _PEER_HEADS
    return pl.pallas_call(
        _finish_body,
        grid=(n // tm,),
        in_specs=[row(_SSD_INNER)] + attn_specs + [row(_GATE_W), row(_D),
                  const((_SSD_INNER, _D)), const((_ATT_OUT, _D)), const((_D, _D)), const((1, _D)),
                  const((_D, nq * _PEER_KEYS))],
        out_specs=[row(_D), row(_D), pl.BlockSpec((nq, tm, _PEER_KEYS), lambda i: (0, i, 0))],
        out_shape=[jax.ShapeDtypeStruct((n, _D), _F32), jax.ShapeDtypeStruct((n, _D), _BF16),
                   jax.ShapeDtypeStruct((nq, n, _PEER_KEYS), _BF16)],
        scratch_shapes=[pltpu.VMEM((2 * len(_ATT_GROUPS) * (_ATT_OUT // _LANES), tm, _LANES), _F32)],
        compiler_params=_cparams("parallel"),
        name="finish",
    )(yn, *attn, gates, x2d, wa, wb, wo, ng, wq)


_ROUTE_TB = 1024
_ROUTE_SUB = _ROUTE_TB // _LANES
_ROUTE_CHAINS = 4


def _sweep_top(ref, nrows, k, codes_ref=None):
    sub = _ROUTE_SUB
    neg = jnp.full((sub, _LANES), -jnp.inf, _F32)
    vals, outs = [], []
    prev = None
    for r in range(k):
        best = [None] * _ROUTE_CHAINS
        for row in range(nrows):
            v = ref[row * sub:(row + 1) * sub, :]
            if prev is not None:
                v = jnp.where(prev == row, neg, v)
                ref[row * sub:(row + 1) * sub, :] = v
            c = row % _ROUTE_CHAINS
            rid = jnp.full((sub, _LANES), row, jnp.int32)
            code = codes_ref[row * sub:(row + 1) * sub, :] if codes_ref is not None else None
            if best[c] is None:
                best[c] = (v, rid, code)
            else:
                bv, bk, bc = best[c]
                better = v > bv
                best[c] = (jnp.maximum(bv, v), jnp.where(better, rid, bk),
                           jnp.where(better, code, bc) if code is not None else None)
        acc = best[0]
        for c in range(1, min(_ROUTE_CHAINS, nrows)):
            av, ak, ac = acc
            bv, bk, bc = best[c]
            take = jnp.logical_or(bv > av, jnp.logical_and(bv == av, bk < ak))
            acc = (jnp.maximum(av, bv), jnp.where(take, bk, ak), jnp.where(take, bc, ac) if ac is not None else None)
        vals.append(acc[0])
        outs.append(acc[2] if codes_ref is not None else acc[1])
        prev = acc[1]
    return vals, outs


def _route_body(qp_ref, keys_ref, i_ref, j_ref, g_ref, s_ref, cand_ref, code_ref, is_ref, js_ref, gs_ref):
    nt = (((1,), (1,)), ((), ()))
    k = _PEER_TOPK
    sub = _ROUTE_SUB
    pairs = [(a, b) for a in range(k) for b in range(k // (a + 1))]

    def head(h, carry):
        hv, hi = [], []
        for c in range(2):
            for a in range(sub):
                tile = lax.dot_general(keys_ref[2 * h + c], qp_ref[2 * h + c, a * _LANES:(a + 1) * _LANES, :], nt,
                                       preferred_element_type=_F32)
                s_ref[c, pl.ds(a, _PEER_KEYS, stride=sub), :] = tile
            v, i = _sweep_top(s_ref.at[c], _PEER_KEYS, k)
            hv.append(v)
            hi.append(i)
        for p, (a, b) in enumerate(pairs):
            cand_ref[p * sub:(p + 1) * sub, :] = hv[0][a] + hv[1][b]
            code_ref[p * sub:(p + 1) * sub, :] = hi[0][a] * _PEER_KEYS + hi[1][b]
        fs, codes = _sweep_top(cand_ref, len(pairs), k, codes_ref=code_ref)
        e = [jnp.exp(f - fs[0]) for f in fs]
        den = e[0]
        for x in e[1:]:
            den = den + x
        for r in range(k):
            r0 = pl.multiple_of((h * k + r) * sub, sub)
            is_ref[pl.ds(r0, sub), :] = jnp.right_shift(codes[r], 7).astype(_F32)
            js_ref[pl.ds(r0, sub), :] = jnp.bitwise_and(codes[r], _PEER_KEYS - 1).astype(_F32)
            gs_ref[pl.ds(r0, sub), :] = e[r] / den
        return carry

    lax.fori_loop(0, _PEER_HEADS, head, 0)
    slots = _PEER_HEADS * k
    for src, dst in ((is_ref, i_ref), (js_ref, j_ref), (gs_ref, g_ref)):
        for a in range(sub):
            dst[a * _LANES:(a + 1) * _LANES, :] = src[pl.ds(a, slots, stride=sub), :].T


def _peer_route(qp, sub_keys16):
    nq, n, _ = qp.shape
    tq = _ROUTE_TB
    slots = _PEER_HEADS * _PEER_TOPK
    ncand = sum(_PEER_TOPK // (a + 1) for a in range(_PEER_TOPK))
    out = pl.BlockSpec((tq, slots), lambda i: (i, 0))
    return pl.pallas_call(
        _route_body,
        grid=(n // tq,),
        in_specs=[pl.BlockSpec((nq, tq, _PEER_KEYS), lambda i: (0, i, 0)),
                  pl.BlockSpec((nq, _PEER_KEYS, _PEER_KEYS), lambda i: (0, 0, 0))],
        out_specs=[out] * 3,
        out_shape=[jax.ShapeDtypeStruct((n, slots), _F32)] * 3,
        scratch_shapes=[pltpu.VMEM((2, _PEER_KEYS * _ROUTE_SUB, _LANES), _F32),
                        pltpu.VMEM((ncand * _ROUTE_SUB, _LANES), _F32),
                        pltpu.VMEM((ncand * _ROUTE_SUB, _LANES), jnp.int32)]
        + [pltpu.VMEM((slots * _ROUTE_SUB, _LANES), _F32)] * 3,
        compiler_params=_cparams("parallel"),
        name="peer_route",
    )(qp, sub_keys16)


_PEER_TM = 512
_PEER_HALF = _PEER_TM // 2
_PEER_EB = 2048
_PEER_CW = 256
_PEER_PITCH = _PEER_HALF + 8
_PEER_UNROLL = 32


def _peer_body(h_ref, i_ref, j_ref, g_ref, x1_ref, u_ref, v_ref, o_ref, gate_ref, w_ref):
    e = pl.program_id(1)
    slots = _PEER_HEADS * _PEER_TOPK
    nt = (((1,), (1,)), ((), ()))
    hi_mask = jnp.uint32(0xFFFF0000)

    @pl.when(e == 0)
    def _():
        o_ref[...] = x1_ref[...]
        sub = lax.broadcasted_iota(jnp.int32, (_PEER_KEYS, slots), 0).astype(_F32).astype(_BF16)
        one = jnp.ones((), _BF16)
        zero = jnp.zeros((), _BF16)

        def gate_bits(n):
            irow = i_ref[pl.ds(n, 1), :].astype(_BF16)
            jrow = j_ref[pl.ds(n, 1), :].astype(_BF16)
            grow = g_ref[pl.ds(n, 1), :].astype(_BF16)
            p_t = jnp.where(sub == irow, one, zero)
            q_t = jnp.where(sub == jrow, grow, zero)
            gm = lax.dot_general(p_t, q_t, nt, preferred_element_type=_F32)
            return pltpu.bitcast(gm, jnp.uint32)

        def pair(n, carry):
            lo = jnp.right_shift(gate_bits(n), jnp.uint32(16))
            hi = jnp.bitwise_and(gate_bits(n + _PEER_HALF), hi_mask)
            gate_ref[pl.ds(n, _PEER_KEYS, stride=_PEER_PITCH), :] = jnp.bitwise_or(lo, hi)
            return carry

        lax.fori_loop(0, _PEER_HALF, pair, 0, unroll=_PEER_UNROLL)

    rows_per_chunk = _PEER_CW // _PEER_KEYS
    h = h_ref[...]
    for c in range(_PEER_EB // _PEER_CW):
        hmat = lax.dot_general(h, u_ref[c * _PEER_CW:(c + 1) * _PEER_CW, :], nt, preferred_element_type=_F32)
        act = 0.5 * hmat * (1.0 + lax.erf(hmat * (1.0 / math.sqrt(2.0))))
        cols = []
        for r in range(rows_per_chunk):
            row = e * (_PEER_EB // _PEER_KEYS) + c * rows_per_chunk + r
            word = gate_ref[pl.ds(pl.multiple_of(row * _PEER_PITCH, 8), _PEER_HALF), :]
            lo = pltpu.bitcast(jnp.left_shift(word, jnp.uint32(16)), _F32)
            hi = pltpu.bitcast(jnp.bitwise_and(word, hi_mask), _F32)
            cols.append(jnp.concatenate([lo, hi], axis=0))
        gate = jnp.concatenate(cols, axis=1)
        w_ref[:, c * _PEER_CW:(c + 1) * _PEER_CW] = (gate * act).astype(_BF16)
    o_ref[...] += jnp.dot(w_ref[...], v_ref[...], preferred_element_type=_F32)


def _peer_dense(h2, ri, rj, rg, x1, u16, v16):
    n = h2.shape[0]
    tm, eb = _PEER_TM, _PEER_EB
    slots = _PEER_HEADS * _PEER_TOPK
    nexp = u16.shape[0]
    tok = lambda w: pl.BlockSpec((tm, w), lambda i, e: (i, 0))
    return pl.pallas_call(
        _peer_body,
        grid=(n // tm, nexp // eb),
        in_specs=[tok(_D), tok(slots), tok(slots), tok(slots), tok(_D),
                  pl.BlockSpec((eb, _D), lambda i, e: (e, 0)),
                  pl.BlockSpec((eb, _D), lambda i, e: (e, 0))],
        out_specs=tok(_D),
        out_shape=jax.ShapeDtypeStruct((n, _D), _F32),
        scratch_shapes=[pltpu.VMEM((_PEER_KEYS * _PEER_PITCH, _PEER_KEYS), jnp.uint32),
                        pltpu.VMEM((tm, eb), _BF16)],
        compiler_params=_cparams("parallel", "arbitrary"),
        name="peer_dense",
    )(h2, ri, rj, rg, x1, u16, v16)


def _prepare(norm_mix_g, w_in, conv_w, conv_b, dt_bias, a_log, d_skip, ssd_norm_g, w_a,
             q_norm_g, k_norm_g, w_b, w_o, norm_ffn_g, w_query, sub_keys, expert_u, expert_v):
    xbc_end = _SSD_INNER + _CONV_DIM
    dt_end = xbc_end + 2 * _SSD_HEADS
    pad_dt = lambda a: jnp.pad(a.astype(_F32).reshape(1, 2 * _SSD_HEADS), ((0, 0), (0, _DT_W - 2 * _SSD_HEADS)))
    row = lambda a: a.astype(_F32)[None, :]
    return dict(
        w_perm=jnp.concatenate(
            [w_in[:, :xbc_end], w_in[:, dt_end:], w_in[:, xbc_end:dt_end],
             jnp.zeros((_D, _DT_W - 2 * _SSD_HEADS), w_in.dtype)], axis=1).astype(_BF16),
        norm_mix_g=row(norm_mix_g), conv_w=conv_w.astype(_F32), conv_b=row(conv_b),
        dt_bias=pad_dt(dt_bias), a_log=pad_dt(a_log), d_skip=jnp.repeat(d_skip.astype(_F32), _SSD_HEAD_DIM)[None, :],
        ssd_norm_g=row(ssd_norm_g), q_norm_g=q_norm_g, k_norm_g=k_norm_g,
        w_a=w_a.astype(_BF16), w_b=w_b.astype(_BF16), w_o=w_o.astype(_BF16), norm_ffn_g=row(norm_ffn_g),
        w_query=w_query.astype(_BF16),
        keys16=sub_keys.reshape(2 * _PEER_HEADS, _PEER_KEYS, sub_keys.shape[-1]).astype(_BF16),
        u16=expert_u.astype(_BF16), v16=expert_v.astype(_BF16),
    )


def _layer(x, w):
    bsz, seq, d_model = x.shape
    n = bsz * seq
    max_dil = max(d for _, d in _ATT_GROUPS)
    assert d_model == _D and seq % _TOKEN_TILE == 0 and seq % _SSD_STEP == 0, x.shape
    assert seq % (max_dil * _QSUB * _QB) == 0 and _TOKEN_TILE % (max_dil * _HALO) == 0, x.shape
    assert n % _ROUTE_TB == 0 and n % _PEER_TM == 0, x.shape
    x2d = x.reshape(n, _D)
    z, xbc, qkv, gates, dt = _in_proj(x2d, w["norm_mix_g"], w["w_perm"], tm=_TOKEN_TILE)
    qkv_groups = _qk_prep(qkv, w["q_norm_g"], w["k_norm_g"], bsz, seq, tm=_TOKEN_TILE)
    act, yf = _ssd_fwd(xbc, dt, w["conv_w"], w["conv_b"], w["dt_bias"], w["a_log"], w["d_skip"], bsz, seq)
    yn = _ssd_bwd(act, dt, w["dt_bias"], w["a_log"], yf, z, w["ssd_norm_g"], bsz, seq)
    attn = []
    for gi, (window, dil) in enumerate(_ATT_GROUPS):
        assert window // (2 * dil) == _ATT_RADIUS
        attn.extend(_attention_group(*qkv_groups[3 * gi:3 * gi + 3], bsz, seq, dil))
    x1, h2, qp = _finish(yn.reshape(n, _SSD_INNER), attn, gates, x2d, w["w_a"], w["w_b"], w["w_o"],
                         w["norm_ffn_g"], w["w_query"], seq, tm=_TOKEN_TILE)
    ri, rj, rg = _peer_route(qp, w["keys16"])
    out = _peer_dense(h2, ri, rj, rg, x1, w["u16"], w["v16"])
    return out.reshape(bsz, seq, _D)


def kernel(x_prompt, x_sample, norm_mix_g, w_in, conv_w, conv_b, dt_bias, a_log, d_skip, ssd_norm_g, w_a,
           q_norm_g, k_norm_g, w_b, w_o, norm_ffn_g, w_query, sub_keys, expert_u, expert_v):
    weights = (norm_mix_g, w_in, conv_w, conv_b, dt_bias, a_log, d_skip, ssd_norm_g, w_a,
               q_norm_g, k_norm_g, w_b, w_o, norm_ffn_g, w_query, sub_keys, expert_u, expert_v)
    outs = [x_prompt, x_sample]
    for l in range(w_in.shape[0]):
        w = _prepare(*(a[l] for a in weights))
        outs = [_layer(x, w) for x in outs]
    return tuple(outs)
```
